```python
import math
import jax, jax.numpy as jnp
from jax import lax
import numpy as np

D_MODEL = 1024
BATCH = 16
SEQ = 2048
DEPTH = 1
DEC_BATCH = 1
DEC_SEQ = 16384
PAST_LEN = 128

EPS = 1e-6
POOL_WINDOWS = (2, 4, 8, 16)
POOL_GROUPS = len(POOL_WINDOWS)
POOL_WIDTH = D_MODEL
POOL_GDIM = POOL_WIDTH // POOL_GROUPS
FOUR_GROUPS = 4
FOUR_WIDTH = D_MODEL // 2
FOUR_GDIM = FOUR_WIDTH // FOUR_GROUPS
ATTN_HEADS = 4
ATTN_WIDTH = D_MODEL // 2
HEAD_DIM = ATTN_WIDTH // ATTN_HEADS
N_MEM = 256
N_BRANCH = 3
IN_SPLITS = (POOL_WIDTH, POOL_WIDTH, FOUR_WIDTH, FOUR_WIDTH, ATTN_WIDTH, ATTN_WIDTH, N_BRANCH * D_MODEL)
IN_WIDTH = sum(IN_SPLITS)

kernel_name = "gated_pool_fourier_memxattn_encoder"


def rmsnorm(x, g):
    xf = x.astype(jnp.float32)
    r = lax.rsqrt(jnp.mean(xf * xf, axis=-1, keepdims=True) + EPS)
    return (xf * r * g.astype(jnp.float32)).astype(x.dtype)


def split_cols(z):
    offs = np.cumsum(IN_SPLITS)[:-1]
    return jnp.split(z, offs, axis=-1)


def pool_mixer(u, w_grp, scale):
    B, S, _ = u.shape
    ug = u.reshape(B, S, POOL_GROUPS, POOL_GDIM).astype(jnp.float32)
    cs = jnp.concatenate([jnp.zeros((B, 1, POOL_GROUPS, POOL_GDIM), jnp.float32), jnp.cumsum(ug, axis=1)], axis=1)
    t = np.arange(S)
    outs = []
    for g, w in enumerate(POOL_WINDOWS):
        half = w // 2
        lo = np.clip(t - half, 0, S)
        hi = np.clip(t + half, 0, S)
        cnt = jnp.asarray((hi - lo).astype(np.float32))[None, :, None]
        csg = cs[:, :, g]
        win_sum = jnp.take(csg, jnp.asarray(hi), axis=1) - jnp.take(csg, jnp.asarray(lo), axis=1)
        outs.append(win_sum / cnt - ug[:, :, g])
    p = jnp.stack(outs, axis=2).astype(u.dtype)
    y = jnp.einsum('bsgc,gcd->bsgd', p, w_grp).reshape(B, S, POOL_WIDTH)
    return y * scale


def fourier_mixer(u, w_grp):
    B, S, _ = u.shape
    uh = u.reshape(B, S, FOUR_GROUPS, FOUR_GDIM).astype(jnp.float32)
    f = jnp.fft.fftn(uh, axes=(1, 3), norm='ortho').real.astype(u.dtype)
    return jnp.einsum('bsgc,gcd->bsgd', f, w_grp).reshape(B, S, FOUR_WIDTH)


def memory_xattn(q, mem_n, w_kv):
    B, S, _ = q.shape
    kv = mem_n @ w_kv
    k, v = jnp.split(kv, 2, axis=-1)
    qh = q.reshape(B, S, ATTN_HEADS, HEAD_DIM)
    kh = k.reshape(B, N_MEM, ATTN_HEADS, HEAD_DIM)
    vh = v.reshape(B, N_MEM, ATTN_HEADS, HEAD_DIM)
    s = jnp.einsum('bshd,bmhd->bhsm', qh, kh).astype(jnp.float32) * (1.0 / math.sqrt(HEAD_DIM))
    p = jax.nn.softmax(s, axis=-1).astype(vh.dtype)
    o = jnp.einsum('bhsm,bmhd->bshd', p, vh)
    return o.reshape(B, S, ATTN_WIDTH)


def trunk(x, mem, norm_in, norm_mem, w_in, w_pool_grp, pool_scale, w_four_grp, w_kv,
          w_pool_out, w_four_out, w_attn_out, b_gate, w_o, norm_f):
    for l in range(DEPTH):
        xn = rmsnorm(x, norm_in[l])
        mn = rmsnorm(mem, norm_mem[l])
        pv, pg, fv, fg, q, ag, mg = split_cols(xn @ w_in[l])
        ya = (pool_mixer(pv, w_pool_grp[l], pool_scale[l]) * jax.nn.silu(pg)) @ w_pool_out[l]
        yb = (fourier_mixer(fv, w_four_grp[l]) * jax.nn.silu(fg)) @ w_four_out[l]
        yc = (memory_xattn(q, mn, w_kv[l]) * jax.nn.silu(ag)) @ w_attn_out[l]
        ga, gb, gc = jnp.split(jax.nn.sigmoid(mg + b_gate[l]), N_BRANCH, axis=-1)
        merged = ga * ya + gb * yb + gc * yc
        x = x + merged @ w_o[l]
    return rmsnorm(x, norm_f)


def setup_inputs(seed: int = 0) -> dict:
    key = jax.random.key(seed)
    ks = jax.random.split(key, 20)
    f32 = jnp.float32
    nrm = lambda k, shp, s: jax.random.normal(k, shp, f32) * s
    return {
        'x_prompt': nrm(ks[0], (BATCH, SEQ, D_MODEL), 1.0),
        'x_sample': nrm(ks[1], (DEC_BATCH, DEC_SEQ, D_MODEL), 1.0),
        'mem_prompt': nrm(ks[2], (BATCH, N_MEM, D_MODEL), 1.0),
        'mem_sample': nrm(ks[3], (DEC_BATCH, N_MEM, D_MODEL), 1.0),
        'norm_in': 1.0 + nrm(ks[4], (DEPTH, D_MODEL), 0.05),
        'norm_mem': 1.0 + nrm(ks[5], (DEPTH, D_MODEL), 0.05),
        'w_in': nrm(ks[6], (DEPTH, D_MODEL, IN_WIDTH), D_MODEL ** -0.5),
        'w_pool_grp': nrm(ks[7], (DEPTH, POOL_GROUPS, POOL_GDIM, POOL_GDIM), POOL_GDIM ** -0.5),
        'pool_scale': 1.0 + nrm(ks[8], (DEPTH, POOL_WIDTH), 0.05),
        'w_four_grp': nrm(ks[9], (DEPTH, FOUR_GROUPS, FOUR_GDIM, FOUR_GDIM), FOUR_GDIM ** -0.5),
        'w_kv': nrm(ks[10], (DEPTH, D_MODEL, 2 * ATTN_WIDTH), D_MODEL ** -0.5),
        'w_pool_out': nrm(ks[11], (DEPTH, POOL_WIDTH, D_MODEL), POOL_WIDTH ** -0.5),
        'w_four_out': nrm(ks[12], (DEPTH, FOUR_WIDTH, D_MODEL), FOUR_WIDTH ** -0.5),
        'w_attn_out': nrm(ks[13], (DEPTH, ATTN_WIDTH, D_MODEL), ATTN_WIDTH ** -0.5),
        'b_gate': nrm(ks[14], (DEPTH, N_BRANCH * D_MODEL), 0.1),
        'w_o': nrm(ks[15], (DEPTH, D_MODEL, D_MODEL), D_MODEL ** -0.5),
        'norm_f': 1.0 + nrm(ks[16], (D_MODEL,), 0.05),
    }


def reference(x_prompt, x_sample, mem_prompt, mem_sample, norm_in, norm_mem, w_in, w_pool_grp,
              pool_scale, w_four_grp, w_kv, w_pool_out, w_four_out, w_attn_out, b_gate, w_o, norm_f):
    y_prompt = trunk(x_prompt, mem_prompt, norm_in, norm_mem, w_in, w_pool_grp, pool_scale, w_four_grp,
                     w_kv, w_pool_out, w_four_out, w_attn_out, b_gate, w_o, norm_f)
    y_sample = trunk(x_sample, mem_sample, norm_in, norm_mem, w_in, w_pool_grp, pool_scale, w_four_grp,
                     w_kv, w_pool_out, w_four_out, w_attn_out, b_gate, w_o, norm_f)
    return (y_prompt, y_sample)
```

```python
import functools
import math

import jax
import jax.numpy as jnp
import numpy as np
from jax import lax
from jax.experimental import pallas as pl
from jax.experimental.pallas import tpu as pltpu

D_MODEL = 1024
EPS = 1e-6
POOL_WINDOWS = (2, 4, 8, 16)
POOL_GDIM = 256
POOL_HALO = 8
FOUR_GROUPS = 4
FOUR_GDIM = 128
FOUR_WIDTH = FOUR_GROUPS * FOUR_GDIM
ATTN_HEADS = 4
HEAD_DIM = 128
ATTN_WIDTH = ATTN_HEADS * HEAD_DIM
N_MEM = 256
OFF_PV, OFF_PG, OFF_FV, OFF_FG, OFF_Q, OFF_AG, OFF_MG = 0, 1024, 2048, 2560, 3072, 3584, 4096

SUBLANES = 8
DFT_N1 = 32
TOKEN_TILE = 256
VMEM_LIMIT = 56 * 1024 * 1024

BF16 = jnp.bfloat16
F32 = jnp.float32


def _dot(a, b):
    return jnp.dot(a, b, preferred_element_type=F32)


def _rmsnorm(x, g):
    r = lax.rsqrt(jnp.mean(x * x, axis=-1, keepdims=True) + EPS)
    return x * r * g


def _sigmoid(x):
    return 1.0 / (1.0 + jnp.exp(-x))


def _silu(x):
    return x * _sigmoid(x)


def _const_spec(shape):
    nd = len(shape)
    return pl.BlockSpec(shape, lambda *_: (0,) * nd, pipeline_mode=pl.Buffered(1))


@functools.cache
def _dft_constants(seq):
    n1 = DFT_N1
    n2 = seq // n1
    c = np.arange(FOUR_GDIM)
    ang = 2.0 * np.pi * np.outer(c, c) / FOUR_GDIM
    chan = np.concatenate([np.cos(ang), -np.sin(ang)], axis=1)
    k1 = np.arange(n1)
    psi = 2.0 * np.pi * np.outer(k1, k1) / n1
    eye = np.eye(SUBLANES)
    mr = np.kron(np.cos(psi), eye)
    mi = np.kron(-np.sin(psi), eye)
    stage_a = np.stack([np.concatenate([mr, -mi], axis=1),
                        np.concatenate([mi, mr], axis=1)], axis=0)
    j = np.arange(n2 // SUBLANES)
    r = np.arange(SUBLANES)
    s2 = (SUBLANES * j[:, None, None] + r[None, None, :])
    phi = 2.0 * np.pi * k1[None, :, None] * s2 / seq
    phi = phi.reshape(len(j), n1 * SUBLANES, 1)
    tw = np.concatenate([np.broadcast_to(np.cos(phi), phi.shape[:2] + (128,)),
                         np.broadcast_to(-np.sin(phi), phi.shape[:2] + (128,))], axis=2)
    k2 = np.arange(n2)
    th = 2.0 * np.pi * np.outer(k2, k2) / n2
    stage_b = np.concatenate([np.cos(th), np.sin(th)], axis=1)
    return tuple(np.asarray(a, np.float32) for a in (chan, stage_a, tw, stage_b))


def _kv_kernel(mem_ref, g_ref, wkv_ref, kv_ref):
    mn = _rmsnorm(mem_ref[...], g_ref[...]).astype(BF16)
    kv_ref[...] = _dot(mn, wkv_ref[...]).astype(BF16)


def _kv_call(mem, norm_mem, w_kv_bf):
    b = mem.shape[0]
    return pl.pallas_call(
        _kv_kernel,
        grid=(b,),
        in_specs=[pl.BlockSpec((None, N_MEM, D_MODEL), lambda i: (i, 0, 0)),
                  _const_spec((1, D_MODEL)),
                  _const_spec((D_MODEL, 2 * ATTN_WIDTH))],
        out_specs=pl.BlockSpec((None, N_MEM, 2 * ATTN_WIDTH), lambda i: (i, 0, 0)),
        out_shape=jax.ShapeDtypeStruct((b, N_MEM, 2 * ATTN_WIDTH), BF16),
        compiler_params=pltpu.CompilerParams(dimension_semantics=("arbitrary",)),
        name="kv_proj",
    )(mem, norm_mem, w_kv_bf)


def _fourier_a_kernel(x_ref, g_ref, wfv_ref, chan_ref, sta_ref, tw_ref, t_ref):
    rows = DFT_N1 * SUBLANES
    x = x_ref[...].reshape(rows, D_MODEL)
    xn = _rmsnorm(x, g_ref[...]).astype(BF16)
    fv = _dot(xn, wfv_ref[...]).astype(BF16)
    zr, zi = [], []
    for h in range(FOUR_GROUPS):
        z = _dot(fv[:, h * FOUR_GDIM:(h + 1) * FOUR_GDIM], chan_ref[...])
        zr.append(z[:, :FOUR_GDIM])
        zi.append(z[:, FOUR_GDIM:])
    zcat = jnp.concatenate([jnp.concatenate(zr, axis=1), jnp.concatenate(zi, axis=1)], axis=0).astype(BF16)
    tr = _dot(sta_ref[0], zcat)
    ti = _dot(sta_ref[1], zcat)
    twr = tw_ref[:, :FOUR_GDIM]
    twi = tw_ref[:, FOUR_GDIM:]
    for h in range(FOUR_GROUPS):
        sl = slice(h * FOUR_GDIM, (h + 1) * FOUR_GDIM)
        a, b = tr[:, sl], ti[:, sl]
        t_ref[:, :, h * FOUR_GDIM:(h + 1) * FOUR_GDIM] = (a * twr - b * twi).reshape(DFT_N1, SUBLANES, FOUR_GDIM)
        t_ref[:, :, FOUR_WIDTH + h * FOUR_GDIM:FOUR_WIDTH + (h + 1) * FOUR_GDIM] = (
            (a * twi + b * twr).reshape(DFT_N1, SUBLANES, FOUR_GDIM))


def _fourier_a_call(x, norm_in, w_fv_bf, chan, stage_a, tw):
    b, seq, _ = x.shape
    n1, n2 = DFT_N1, seq // DFT_N1
    rows = n1 * SUBLANES
    x4 = x.reshape(b, n1, n2, D_MODEL)
    return pl.pallas_call(
        _fourier_a_kernel,
        grid=(b, n2 // SUBLANES),
        in_specs=[pl.BlockSpec((None, n1, SUBLANES, D_MODEL), lambda i, j: (i, 0, j, 0)),
                  _const_spec((1, D_MODEL)),
                  _const_spec((D_MODEL, FOUR_WIDTH)),
                  _const_spec((FOUR_GDIM, 2 * FOUR_GDIM)),
                  _const_spec((2, rows, 2 * rows)),
                  pl.BlockSpec((None, rows, 2 * FOUR_GDIM), lambda i, j: (j, 0, 0))],
        out_specs=pl.BlockSpec((None, n1, SUBLANES, 2 * FOUR_WIDTH), lambda i, j: (i, 0, j, 0)),
        out_shape=jax.ShapeDtypeStruct((b, n1, n2, 2 * FOUR_WIDTH), F32),
        compiler_params=pltpu.CompilerParams(dimension_semantics=("arbitrary", "arbitrary")),
        name="fourier_a",
    )(x4, norm_in, w_fv_bf, chan, stage_a, tw)


def _fourier_b_kernel(t_ref, stb_ref, f_ref, *, k1_per_step, scale):
    for k in range(k1_per_step):
        t = t_ref[k]
        tcat = jnp.concatenate([t[:, :FOUR_WIDTH], t[:, FOUR_WIDTH:]], axis=0).astype(BF16)
        f_ref[k] = _dot(stb_ref[...], tcat) * scale


def _fourier_b_call(t, stage_b, seq):
    b, n1, n2, _ = t.shape
    k1_per_step = max(1, 512 // n2)
    kern = functools.partial(_fourier_b_kernel, k1_per_step=k1_per_step,
                             scale=1.0 / math.sqrt(seq * FOUR_GDIM))
    return pl.pallas_call(
        kern,
        grid=(b, n1 // k1_per_step),
        in_specs=[pl.BlockSpec((None, k1_per_step, n2, 2 * FOUR_WIDTH), lambda i, j: (i, j, 0, 0)),
                  _const_spec((n2, 2 * n2))],
        out_specs=pl.BlockSpec((None, k1_per_step, n2, FOUR_WIDTH), lambda i, j: (i, j, 0, 0)),
        out_shape=jax.ShapeDtypeStruct((b, n1, n2, FOUR_WIDTH), F32),
        compiler_params=pltpu.CompilerParams(dimension_semantics=("arbitrary", "arbitrary")),
        name="fourier_b",
    )(t, stage_b)


def _main_kernel(xm_ref, xp_ref, xq_ref, f_ref, kv_ref, gin_ref, win_ref, wpg_ref, psc_ref, wfg_ref,
                 wpo_ref, wfo_ref, wao_ref, bg_ref, wo_ref, gf_ref, out_ref, uext_ref, *, seq, tile):
    i = pl.program_id(1)
    last = pl.num_programs(1) - 1
    gin = gin_ref[...]
    x = xm_ref[...]
    xn = _rmsnorm(x, gin).astype(BF16)

    def proj(off, width, lhs=xn):
        return _dot(lhs, win_ref[:, off:off + width])

    def gate(branch):
        off = branch * D_MODEL
        return _sigmoid(proj(OFF_MG + off, D_MODEL) + bg_ref[:, off:off + D_MODEL])

    halo = jnp.concatenate([xp_ref[...], xq_ref[...]], axis=0)
    pvh = proj(OFF_PV, D_MODEL, _rmsnorm(halo, gin).astype(BF16))
    pv = proj(OFF_PV, D_MODEL)
    uext_ref[0:POOL_HALO, :] = jnp.where(i > 0, pvh[:POOL_HALO], 0.0)
    uext_ref[POOL_HALO:POOL_HALO + tile, :] = pv
    uext_ref[POOL_HALO + tile:, :] = jnp.where(i < last, pvh[POOL_HALO:], 0.0)
    t = i * tile + lax.broadcasted_iota(jnp.int32, (tile, 1), 0)
    ys = []
    for g, w in enumerate(POOL_WINDOWS):
        half = w // 2
        lanes = slice(g * POOL_GDIM, (g + 1) * POOL_GDIM)
        win = uext_ref[POOL_HALO - half:POOL_HALO - half + tile, lanes]
        for jj in range(1, w):
            win = win + uext_ref[POOL_HALO - half + jj:POOL_HALO - half + jj + tile, lanes]
        cnt = (jnp.minimum(t + half, seq) - jnp.maximum(t - half, 0)).astype(F32)
        p = (win / cnt - pv[:, lanes]).astype(BF16)
        ys.append(_dot(p, wpg_ref[g]))
    y = jnp.concatenate(ys, axis=1) * psc_ref[...]
    ya = _dot((y * _silu(proj(OFF_PG, D_MODEL))).astype(BF16), wpo_ref[...])
    merged = gate(0) * ya

    f = f_ref[...].astype(BF16)
    yb = jnp.concatenate([_dot(f[:, h * FOUR_GDIM:(h + 1) * FOUR_GDIM], wfg_ref[h])
                          for h in range(FOUR_GROUPS)], axis=1)
    yb = _dot((yb * _silu(proj(OFF_FG, FOUR_WIDTH))).astype(BF16), wfo_ref[...])
    merged = merged + gate(1) * yb

    q = proj(OFF_Q, ATTN_WIDTH).astype(BF16)
    os_ = []
    for h in range(ATTN_HEADS):
        kh = kv_ref[:, h * HEAD_DIM:(h + 1) * HEAD_DIM]
        vh = kv_ref[:, ATTN_WIDTH + h * HEAD_DIM:ATTN_WIDTH + (h + 1) * HEAD_DIM]
        s = lax.dot_general(q[:, h * HEAD_DIM:(h + 1) * HEAD_DIM], kh, (((1,), (1,)), ((), ())),
                            preferred_element_type=F32) * (1.0 / math.sqrt(HEAD_DIM))
        e = jnp.exp(s - jnp.max(s, axis=-1, keepdims=True))
        l = jnp.sum(e, axis=-1, keepdims=True)
        os_.append(_dot(e.astype(BF16), vh) / l)
    o = jnp.concatenate(os_, axis=1)
    yc = _dot((o * _silu(proj(OFF_AG, ATTN_WIDTH))).astype(BF16), wao_ref[...])
    merged = merged + gate(2) * yc

    hres = x + _dot(merged.astype(BF16), wo_ref[...])
    out_ref[...] = _rmsnorm(hres, gf_ref[...])


def _main_call(x, f, kv, norm_in, w_in_bf, w_pool_grp_bf, pool_scale, w_four_grp_bf,
               w_pool_out_bf, w_four_out_bf, w_attn_out_bf, b_gate, w_o_bf, norm_f):
    b, seq, _ = x.shape
    tile = TOKEN_TILE
    nt = seq // tile
    hb = tile // POOL_HALO
    nhb = seq // POOL_HALO
    kern = functools.partial(_main_kernel, seq=seq, tile=tile)
    return pl.pallas_call(
        kern,
        grid=(b, nt),
        in_specs=[pl.BlockSpec((None, tile, D_MODEL), lambda bi, i: (bi, i, 0)),
                  pl.BlockSpec((None, POOL_HALO, D_MODEL), lambda bi, i: (bi, jnp.maximum(i * hb - 1, 0), 0)),
                  pl.BlockSpec((None, POOL_HALO, D_MODEL),
                               lambda bi, i: (bi, jnp.minimum((i + 1) * hb, nhb - 1), 0)),
                  pl.BlockSpec((None, tile, FOUR_WIDTH), lambda bi, i: (bi, i, 0)),
                  pl.BlockSpec((None, N_MEM, 2 * ATTN_WIDTH), lambda bi, i: (bi, 0, 0)),
                  _const_spec((1, D_MODEL)),
                  _const_spec(w_in_bf.shape),
                  _const_spec(w_pool_grp_bf.shape),
                  _const_spec((1, D_MODEL)),
                  _const_spec(w_four_grp_bf.shape),
                  _const_spec(w_pool_out_bf.shape),
                  _const_spec(w_four_out_bf.shape),
                  _const_spec(w_attn_out_bf.shape),
                  _const_spec((1, 3 * D_MODEL)),
                  _const_spec(w_o_bf.shape),
                  _const_spec((1, D_MODEL))],
        out_specs=pl.BlockSpec((None, tile, D_MODEL), lambda bi, i: (bi, i, 0)),
        out_shape=jax.ShapeDtypeStruct((b, seq, D_MODEL), F32),
        scratch_shapes=[pltpu.VMEM((tile + 2 * POOL_HALO, D_MODEL), F32)],
        compiler_params=pltpu.CompilerParams(dimension_semantics=("arbitrary", "arbitrary"),
                                             vmem_limit_bytes=VMEM_LIMIT),
        name="encoder_main",
    )(x, x, x, f, kv, norm_in, w_in_bf, w_pool_grp_bf, pool_scale, w_four_grp_bf,
      w_pool_out_bf, w_four_out_bf, w_attn_out_bf, b_gate, w_o_bf, norm_f)


def _trunk(x, mem, norm_in, norm_mem, w_in_bf, w_fv_bf, w_pool_grp_bf, pool_scale, w_four_grp_bf, w_kv_bf,
           w_pool_out_bf, w_four_out_bf, w_attn_out_bf, b_gate, w_o_bf, norm_f):
    b, seq, _ = x.shape
    chan, stage_a, tw, stage_b = _dft_constants(seq)
    chan, stage_a, stage_b = (jnp.asarray(a).astype(BF16) for a in (chan, stage_a, stage_b))
    kv = _kv_call(mem, norm_mem, w_kv_bf)
    t = _fourier_a_call(x, norm_in, w_fv_bf, chan, stage_a, tw)
    f = _fourier_b_call(t, stage_b, seq)
    f = jnp.swapaxes(f, 1, 2).reshape(b, seq, FOUR_WIDTH)
    return _main_call(x, f, kv, norm_in, w_in_bf, w_pool_grp_bf, pool_scale, w_four_grp_bf,
                      w_pool_out_bf, w_four_out_bf, w_attn_out_bf, b_gate, w_o_bf, norm_f)


def kernel(x_prompt, x_sample, mem_prompt, mem_sample, norm_in, norm_mem, w_in, w_pool_grp, pool_scale,
           w_four_grp, w_kv, w_pool_out, w_four_out, w_attn_out, b_gate, w_o, norm_f):
    assert norm_in.shape[0] == 1, "single-layer trunk"
    w_in_bf = w_in[0].astype(BF16)
    shared = (norm_in, norm_mem, w_in_bf, w_in_bf[:, OFF_FV:OFF_FV + FOUR_WIDTH], w_pool_grp[0].astype(BF16),
              pool_scale, w_four_grp[0].astype(BF16), w_kv[0].astype(BF16), w_pool_out[0].astype(BF16),
              w_four_out[0].astype(BF16), w_attn_out[0].astype(BF16), b_gate, w_o[0].astype(BF16),
              norm_f.reshape(1, D_MODEL))
    return (_trunk(x_prompt, mem_prompt, *shared), _trunk(x_sample, mem_sample, *shared))
```

```python
import functools
import math

import jax
import jax.numpy as jnp
import numpy as np
from jax import lax
from jax.experimental import pallas as pl
from jax.experimental.pallas import tpu as pltpu

D_MODEL = 1024
EPS = 1e-6
POOL_WINDOWS = (2, 4, 8, 16)
POOL_GDIM = 256
POOL_HALO = 8
FOUR_GROUPS = 4
FOUR_GDIM = 128
FOUR_WIDTH = FOUR_GROUPS * FOUR_GDIM
ATTN_HEADS = 4
HEAD_DIM = 128
ATTN_WIDTH = ATTN_HEADS * HEAD_DIM
N_MEM = 256
OFF_PV, OFF_PG, OFF_FV, OFF_FG, OFF_Q, OFF_AG, OFF_MG = 0, 1024, 2048, 2560, 3072, 3584, 4096

SUBLANES = 8
LANES = 128
DFT_N1 = 32
TOKEN_TILE = 512
VMEM_LIMIT = 56 * 1024 * 1024

BF16 = jnp.bfloat16
F32 = jnp.float32


def _dot(a, b):
    return jnp.dot(a, b, preferred_element_type=F32)


def _rmsnorm(x, g):
    r = lax.rsqrt(jnp.mean(x * x, axis=-1, keepdims=True) + EPS)
    return x * r * g


def _sigmoid(x):
    return 0.5 * jnp.tanh(0.5 * x) + 0.5


def _silu(x):
    return x * _sigmoid(x)


def _const_spec(shape):
    nd = len(shape)
    return pl.BlockSpec(shape, lambda *_: (0,) * nd, pipeline_mode=pl.Buffered(1))


@functools.cache
def _dft_constants(seq):
    n1 = DFT_N1
    n2 = seq // n1
    c = np.arange(FOUR_GDIM)
    ang = 2.0 * np.pi * np.outer(c, c) / FOUR_GDIM
    chan = np.concatenate([np.cos(ang), -np.sin(ang)], axis=1)
    k1 = np.arange(n1)
    psi = 2.0 * np.pi * np.outer(k1, k1) / n1
    eye = np.eye(SUBLANES)
    mr = np.kron(np.cos(psi), eye)
    mi = np.kron(-np.sin(psi), eye)
    stage_a = np.stack([np.concatenate([mr, -mi], axis=1),
                        np.concatenate([mi, mr], axis=1)], axis=0)
    j = np.arange(n2 // SUBLANES)
    r = np.arange(SUBLANES)
    s2 = (SUBLANES * j[:, None, None] + r[None, None, :])
    phi = 2.0 * np.pi * k1[None, :, None] * s2 / seq
    phi = phi.reshape(len(j), n1 * SUBLANES, 1)
    tw = np.concatenate([np.broadcast_to(np.cos(phi), phi.shape[:2] + (128,)),
                         np.broadcast_to(-np.sin(phi), phi.shape[:2] + (128,))], axis=2)
    k2 = np.arange(n2)
    th = 2.0 * np.pi * np.outer(k2, k2) / n2
    stage_b = np.concatenate([np.cos(th), np.sin(th)], axis=1)
    return tuple(np.asarray(a, np.float32) for a in (chan, stage_a, tw, stage_b))


def _kv_kernel(mem_ref, g_ref, wkv_ref, kv_ref):
    mn = _rmsnorm(mem_ref[...], g_ref[...]).astype(BF16)
    kv_ref[...] = _dot(mn, wkv_ref[...]).astype(BF16)


def _kv_call(mem, norm_mem, w_kv_bf):
    b = mem.shape[0]
    return pl.pallas_call(
        _kv_kernel,
        grid=(b,),
        in_specs=[pl.BlockSpec((None, N_MEM, D_MODEL), lambda i: (i, 0, 0)),
                  _const_spec((1, D_MODEL)),
                  _const_spec((D_MODEL, 2 * ATTN_WIDTH))],
        out_specs=pl.BlockSpec((None, N_MEM, 2 * ATTN_WIDTH), lambda i: (i, 0, 0)),
        out_shape=jax.ShapeDtypeStruct((b, N_MEM, 2 * ATTN_WIDTH), BF16),
        compiler_params=pltpu.CompilerParams(dimension_semantics=("arbitrary",)),
        name="kv_proj",
    )(mem, norm_mem, w_kv_bf)


def _fourier_a_kernel(x_ref, g_ref, wfv_ref, chan_ref, sta_ref, tw_ref, t_ref):
    rows = DFT_N1 * SUBLANES
    x = x_ref[...].reshape(rows, D_MODEL)
    xn = _rmsnorm(x, g_ref[...]).astype(BF16)
    fv = _dot(xn, wfv_ref[...]).astype(BF16)
    zr, zi = [], []
    for h in range(FOUR_GROUPS):
        z = _dot(fv[:, h * FOUR_GDIM:(h + 1) * FOUR_GDIM], chan_ref[...])
        zr.append(z[:, :FOUR_GDIM])
        zi.append(z[:, FOUR_GDIM:])
    zcat = jnp.concatenate([jnp.concatenate(zr, axis=1), jnp.concatenate(zi, axis=1)], axis=0).astype(BF16)
    tr = _dot(sta_ref[0], zcat)
    ti = _dot(sta_ref[1], zcat)
    twr = tw_ref[:, :FOUR_GDIM]
    twi = tw_ref[:, FOUR_GDIM:]
    for h in range(FOUR_GROUPS):
        sl = slice(h * FOUR_GDIM, (h + 1) * FOUR_GDIM)
        a, b = tr[:, sl], ti[:, sl]
        t_ref[:, :, h * FOUR_GDIM:(h + 1) * FOUR_GDIM] = (a * twr - b * twi).reshape(DFT_N1, SUBLANES, FOUR_GDIM)
        t_ref[:, :, FOUR_WIDTH + h * FOUR_GDIM:FOUR_WIDTH + (h + 1) * FOUR_GDIM] = (
            (a * twi + b * twr).reshape(DFT_N1, SUBLANES, FOUR_GDIM))


def _fourier_a_call(x, norm_in, w_fv_bf, chan, stage_a, tw):
    b, seq, _ = x.shape
    n1, n2 = DFT_N1, seq // DFT_N1
    rows = n1 * SUBLANES
    x4 = x.reshape(b, n1, n2, D_MODEL)
    return pl.pallas_call(
        _fourier_a_kernel,
        grid=(b, n2 // SUBLANES),
        in_specs=[pl.BlockSpec((None, n1, SUBLANES, D_MODEL), lambda i, j: (i, 0, j, 0)),
                  _const_spec((1, D_MODEL)),
                  _const_spec((D_MODEL, FOUR_WIDTH)),
                  _const_spec((FOUR_GDIM, 2 * FOUR_GDIM)),
                  _const_spec((2, rows, 2 * rows)),
                  pl.BlockSpec((None, rows, 2 * FOUR_GDIM), lambda i, j: (j, 0, 0))],
        out_specs=pl.BlockSpec((None, n1, SUBLANES, 2 * FOUR_WIDTH), lambda i, j: (i, 0, j, 0)),
        out_shape=jax.ShapeDtypeStruct((b, n1, n2, 2 * FOUR_WIDTH), F32),
        compiler_params=pltpu.CompilerParams(dimension_semantics=("arbitrary", "arbitrary")),
        name="fourier_a",
    )(x4, norm_in, w_fv_bf, chan, stage_a, tw)


def _fourier_b_kernel(t_ref, stb_ref, f_ref, *, k1_per_step, scale):
    for k in range(k1_per_step):
        t = t_ref[k]
        tcat = jnp.concatenate([t[:, :FOUR_WIDTH], t[:, FOUR_WIDTH:]], axis=0).astype(BF16)
        f_ref[k] = _dot(stb_ref[...], tcat) * scale


def _fourier_b_call(t, stage_b, seq):
    b, n1, n2, _ = t.shape
    k1_per_step = max(1, 512 // n2)
    kern = functools.partial(_fourier_b_kernel, k1_per_step=k1_per_step,
                             scale=1.0 / math.sqrt(seq * FOUR_GDIM))
    return pl.pallas_call(
        kern,
        grid=(b, n1 // k1_per_step),
        in_specs=[pl.BlockSpec((None, k1_per_step, n2, 2 * FOUR_WIDTH), lambda i, j: (i, j, 0, 0)),
                  _const_spec((n2, 2 * n2))],
        out_specs=pl.BlockSpec((None, k1_per_step, n2, FOUR_WIDTH), lambda i, j: (i, j, 0, 0)),
        out_shape=jax.ShapeDtypeStruct((b, n1, n2, FOUR_WIDTH), F32),
        compiler_params=pltpu.CompilerParams(dimension_semantics=("arbitrary", "arbitrary")),
        name="fourier_b",
    )(t, stage_b)


def _main_kernel(xm_ref, xp_ref, xq_ref, f_ref, kv_ref, gin_ref, win_ref, wpg_ref, psc_ref, wfg_ref,
                 wpo_ref, wfo_ref, wao_ref, bg_ref, wo_ref, gf_ref, out_ref, uext_ref, *, seq, tile):
    i = pl.program_id(1)
    last = pl.num_programs(1) - 1
    gin = gin_ref[...]
    x = xm_ref[...]
    xn = _rmsnorm(x, gin).astype(BF16)

    def proj(off, width, lhs=xn):
        return _dot(lhs, win_ref[:, off:off + width])

    def gate(branch):
        off = branch * D_MODEL
        return _sigmoid(proj(OFF_MG + off, D_MODEL) + bg_ref[:, off:off + D_MODEL])

    halo = jnp.concatenate([xp_ref[...], xq_ref[...]], axis=0)
    pvh = proj(OFF_PV, D_MODEL, _rmsnorm(halo, gin).astype(BF16))
    pv = proj(OFF_PV, D_MODEL)
    pv_prev = jnp.where(i > 0, pvh[:POOL_HALO], 0.0)
    pv_next = jnp.where(i < last, pvh[POOL_HALO:], 0.0)
    for s in range(D_MODEL // LANES):
        lanes = slice(s * LANES, (s + 1) * LANES)
        uext_ref[s, 0:POOL_HALO, :] = pv_prev[:, lanes]
        uext_ref[s, POOL_HALO:POOL_HALO + tile, :] = pv[:, lanes]
        uext_ref[s, POOL_HALO + tile:, :] = pv_next[:, lanes]
    t = i * tile + lax.broadcasted_iota(jnp.int32, (tile, 1), 0)
    slabs_per_group = POOL_GDIM // LANES
    ys = []
    for g, w in enumerate(POOL_WINDOWS):
        half = w // 2
        inv_cnt = 1.0 / (jnp.minimum(t + half, seq) - jnp.maximum(t - half, 0)).astype(F32)
        ps = []
        for s in range(g * slabs_per_group, (g + 1) * slabs_per_group):
            win = uext_ref[s, pl.ds(POOL_HALO - half, tile, stride=1), :]
            for jj in range(1, w):
                win = win + uext_ref[s, pl.ds(POOL_HALO - half + jj, tile, stride=1), :]
            ps.append(win * inv_cnt - pv[:, s * LANES:(s + 1) * LANES])
        p = jnp.concatenate(ps, axis=1).astype(BF16)
        ys.append(_dot(p, wpg_ref[g]))
    y = jnp.concatenate(ys, axis=1) * psc_ref[...]
    ya = _dot((y * _silu(proj(OFF_PG, D_MODEL))).astype(BF16), wpo_ref[...])
    merged = gate(0) * ya

    f = f_ref[...].astype(BF16)
    yb = jnp.concatenate([_dot(f[:, h * FOUR_GDIM:(h + 1) * FOUR_GDIM], wfg_ref[h])
                          for h in range(FOUR_GROUPS)], axis=1)
    yb = _dot((yb * _silu(proj(OFF_FG, FOUR_WIDTH))).astype(BF16), wfo_ref[...])
    merged = merged + gate(1) * yb

    q = proj(OFF_Q, ATTN_WIDTH).astype(BF16)
    os_ = []
    for h in range(ATTN_HEADS):
        kh = kv_ref[:, h * HEAD_DIM:(h + 1) * HEAD_DIM]
        vh = kv_ref[:, ATTN_WIDTH + h * HEAD_DIM:ATTN_WIDTH + (h + 1) * HEAD_DIM]
        s = lax.dot_general(q[:, h * HEAD_DIM:(h + 1) * HEAD_DIM], kh, (((1,), (1,)), ((), ())),
                            preferred_element_type=F32) * (1.0 / math.sqrt(HEAD_DIM))
        e = jnp.exp(s - jnp.max(s, axis=-1, keepdims=True))
        l = jnp.sum(e, axis=-1, keepdims=True)
        os_.append(_dot(e.astype(BF16), vh) / l)
    o = jnp.concatenate(os_, axis=1)
    yc = _dot((o * _silu(proj(OFF_AG, ATTN_WIDTH))).astype(BF16), wao_ref[...])
    merged = merged + gate(2) * yc

    hres = x + _dot(merged.astype(BF16), wo_ref[...])
    out_ref[...] = _rmsnorm(hres, gf_ref[...])


def _main_call(x, f, kv, norm_in, w_in_bf, w_pool_grp_bf, pool_scale, w_four_grp_bf,
               w_pool_out_bf, w_four_out_bf, w_attn_out_bf, b_gate, w_o_bf, norm_f):
    b, seq, _ = x.shape
    tile = TOKEN_TILE
    nt = seq // tile
    hb = tile // POOL_HALO
    nhb = seq // POOL_HALO
    kern = functools.partial(_main_kernel, seq=seq, tile=tile)
    return pl.pallas_call(
        kern,
        grid=(b, nt),
        in_specs=[pl.BlockSpec((None, tile, D_MODEL), lambda bi, i: (bi, i, 0)),
                  pl.BlockSpec((None, POOL_HALO, D_MODEL), lambda bi, i: (bi, jnp.maximum(i * hb - 1, 0), 0)),
                  pl.BlockSpec((None, POOL_HALO, D_MODEL),
                               lambda bi, i: (bi, jnp.minimum((i + 1) * hb, nhb - 1), 0)),
                  pl.BlockSpec((None, tile, FOUR_WIDTH), lambda bi, i: (bi, i, 0)),
                  pl.BlockSpec((None, N_MEM, 2 * ATTN_WIDTH), lambda bi, i: (bi, 0, 0)),
                  _const_spec((1, D_MODEL)),
                  _const_spec(w_in_bf.shape),
                  _const_spec(w_pool_grp_bf.shape),
                  _const_spec((1, D_MODEL)),
                  _const_spec(w_four_grp_bf.shape),
                  _const_spec(w_pool_out_bf.shape),
                  _const_spec(w_four_out_bf.shape),
                  _const_spec(w_attn_out_bf.shape),
                  _const_spec((1, 3 * D_MODEL)),
                  _const_spec(w_o_bf.shape),
                  _const_spec((1, D_MODEL))],
        out_specs=pl.BlockSpec((None, tile, D_MODEL), lambda bi, i: (bi, i, 0)),
        out_shape=jax.ShapeDtypeStruct((b, seq, D_MODEL), F32),
        scratch_shapes=[pltpu.VMEM((D_MODEL // LANES, tile + 2 * POOL_HALO, LANES), F32)],
        compiler_params=pltpu.CompilerParams(dimension_semantics=("arbitrary", "arbitrary"),
                                             vmem_limit_bytes=VMEM_LIMIT),
        name="encoder_main",
    )(x, x, x, f, kv, norm_in, w_in_bf, w_pool_grp_bf, pool_scale, w_four_grp_bf,
      w_pool_out_bf, w_four_out_bf, w_attn_out_bf, b_gate, w_o_bf, norm_f)


def _trunk(x, mem, norm_in, norm_mem, w_in_bf, w_fv_bf, w_pool_grp_bf, pool_scale, w_four_grp_bf, w_kv_bf,
           w_pool_out_bf, w_four_out_bf, w_attn_out_bf, b_gate, w_o_bf, norm_f):
    b, seq, _ = x.shape
    chan, stage_a, tw, stage_b = _dft_constants(seq)
    chan, stage_a, stage_b = (jnp.asarray(a).astype(BF16) for a in (chan, stage_a, stage_b))
    kv = _kv_call(mem, norm_mem, w_kv_bf)
    t = _fourier_a_call(x, norm_in, w_fv_bf, chan, stage_a, tw)
    f = _fourier_b_call(t, stage_b, seq)
    f = jnp.swapaxes(f, 1, 2).reshape(b, seq, FOUR_WIDTH)
    return _main_call(x, f, kv, norm_in, w_in_bf, w_pool_grp_bf, pool_scale, w_four_grp_bf,
                      w_pool_out_bf, w_four_out_bf, w_attn_out_bf, b_gate, w_o_bf, norm_f)


def kernel(x_prompt, x_sample, mem_prompt, mem_sample, norm_in, norm_mem, w_in, w_pool_grp, pool_scale,
           w_four_grp, w_kv, w_pool_out, w_four_out, w_attn_out, b_gate, w_o, norm_f):
    assert norm_in.shape[0] == 1, "single-layer trunk"
    w_in_bf = w_in[0].astype(BF16)
    shared = (norm_in, norm_mem, w_in_bf, w_in_bf[:, OFF_FV:OFF_FV + FOUR_WIDTH], w_pool_grp[0].astype(BF16),
              pool_scale, w_four_grp[0].astype(BF16), w_kv[0].astype(BF16), w_pool_out[0].astype(BF16),
              w_four_out[0].astype(BF16), w_attn_out[0].astype(BF16), b_gate, w_o[0].astype(BF16),
              norm_f.reshape(1, D_MODEL))
    return (_trunk(x_prompt, mem_prompt, *shared), _trunk(x_sample, mem_sample, *shared))
```

```python
import functools
import math

import jax
import jax.numpy as jnp
import numpy as np
from jax import lax
from jax.experimental import pallas as pl
from jax.experimental.pallas import tpu as pltpu

D_MODEL = 1024
EPS = 1e-6
POOL_WINDOWS = (2, 4, 8, 16)
POOL_GDIM = 256
POOL_HALO = 8
FOUR_GROUPS = 4
FOUR_GDIM = 128
FOUR_WIDTH = FOUR_GROUPS * FOUR_GDIM
ATTN_HEADS = 4
HEAD_DIM = 128
ATTN_WIDTH = ATTN_HEADS * HEAD_DIM
N_MEM = 256
OFF_PV, OFF_PG, OFF_FV, OFF_FG, OFF_Q, OFF_AG, OFF_MG = 0, 1024, 2048, 2560, 3072, 3584, 4096

SUBLANES = 8
LANES = 128
DFT_N1 = 32
ROW_GROUPS = 2
TOKEN_TILE = 512
VMEM_LIMIT = 56 * 1024 * 1024

BF16 = jnp.bfloat16
F32 = jnp.float32


def _dot(a, b):
    return jnp.dot(a, b, preferred_element_type=F32)


def _rmsnorm(x, g):
    r = lax.rsqrt(jnp.mean(x * x, axis=-1, keepdims=True) + EPS)
    return x * r * g


def _sigmoid(x):
    return 0.5 * jnp.tanh(0.5 * x) + 0.5


def _silu(x):
    return x * _sigmoid(x)


def _const_spec(shape):
    nd = len(shape)
    return pl.BlockSpec(shape, lambda *_: (0,) * nd, pipeline_mode=pl.Buffered(1))


@functools.cache
def _dft_constants(seq):
    n1 = DFT_N1
    n2 = seq // n1
    c = np.arange(FOUR_GDIM)
    ang = 2.0 * np.pi * np.outer(c, c) / FOUR_GDIM
    chan = np.concatenate([np.cos(ang), -np.sin(ang)], axis=1)
    k1 = np.arange(n1)
    psi = 2.0 * np.pi * np.outer(k1, k1) / n1
    eye = np.eye(SUBLANES)
    mr = np.kron(np.cos(psi), eye)
    mi = np.kron(-np.sin(psi), eye)
    stage_a = np.stack([np.concatenate([mr, -mi], axis=1),
                        np.concatenate([mi, mr], axis=1)], axis=0)
    j = np.arange(n2 // SUBLANES)
    r = np.arange(SUBLANES)
    s2 = (SUBLANES * j[:, None, None] + r[None, None, :])
    phi = 2.0 * np.pi * k1[None, :, None] * s2 / seq
    phi = phi.reshape(len(j), n1 * SUBLANES, 1)
    tw = np.concatenate([np.broadcast_to(np.cos(phi), phi.shape[:2] + (128,)),
                         np.broadcast_to(-np.sin(phi), phi.shape[:2] + (128,))], axis=2)
    k2 = np.arange(n2)
    th = 2.0 * np.pi * np.outer(k2, k2) / n2
    stage_b = np.concatenate([np.cos(th), np.sin(th)], axis=1)
    return tuple(np.asarray(a, np.float32) for a in (chan, stage_a, tw, stage_b))


def _kv_kernel(mem_ref, g_ref, wkv_ref, kv_ref):
    mn = _rmsnorm(mem_ref[...], g_ref[...]).astype(BF16)
    kv_ref[...] = _dot(mn, wkv_ref[...]).astype(BF16)


def _kv_call(mem, norm_mem, w_kv_bf):
    b = mem.shape[0]
    return pl.pallas_call(
        _kv_kernel,
        grid=(b,),
        in_specs=[pl.BlockSpec((None, N_MEM, D_MODEL), lambda i: (i, 0, 0)),
                  _const_spec((1, D_MODEL)),
                  _const_spec((D_MODEL, 2 * ATTN_WIDTH))],
        out_specs=pl.BlockSpec((None, N_MEM, 2 * ATTN_WIDTH), lambda i: (i, 0, 0)),
        out_shape=jax.ShapeDtypeStruct((b, N_MEM, 2 * ATTN_WIDTH), BF16),
        compiler_params=pltpu.CompilerParams(dimension_semantics=("arbitrary",)),
        name="kv_proj",
    )(mem, norm_mem, w_kv_bf)


def _fourier_a_kernel(x_ref, g_ref, wfv_ref, chan_ref, sta_ref, tw_ref, t_ref):
    step_rows = ROW_GROUPS * SUBLANES
    x = x_ref[...].reshape(DFT_N1 * step_rows, D_MODEL)
    xn = _rmsnorm(x, g_ref[...]).astype(BF16)
    fv = _dot(xn, wfv_ref[...]).astype(BF16)
    zr, zi = [], []
    for h in range(FOUR_GROUPS):
        z = _dot(fv[:, h * FOUR_GDIM:(h + 1) * FOUR_GDIM], chan_ref[...])
        zr.append(z[:, :FOUR_GDIM])
        zi.append(z[:, FOUR_GDIM:])
    zr = jnp.concatenate(zr, axis=1).reshape(DFT_N1, step_rows, FOUR_WIDTH)
    zi = jnp.concatenate(zi, axis=1).reshape(DFT_N1, step_rows, FOUR_WIDTH)
    kron_rows = DFT_N1 * SUBLANES
    outs = []
    for g in range(ROW_GROUPS):
        rs = slice(g * SUBLANES, (g + 1) * SUBLANES)
        zcat = jnp.concatenate([zr[:, rs, :].reshape(kron_rows, FOUR_WIDTH),
                                zi[:, rs, :].reshape(kron_rows, FOUR_WIDTH)], axis=0).astype(BF16)
        tr = _dot(sta_ref[0], zcat)
        ti = _dot(sta_ref[1], zcat)
        twr = tw_ref[g, :, :FOUR_GDIM]
        twi = tw_ref[g, :, FOUR_GDIM:]
        re, im = [], []
        for h in range(FOUR_GROUPS):
            a, b = tr[:, h * FOUR_GDIM:(h + 1) * FOUR_GDIM], ti[:, h * FOUR_GDIM:(h + 1) * FOUR_GDIM]
            re.append(a * twr - b * twi)
            im.append(a * twi + b * twr)
        outs.append(jnp.concatenate(re + im, axis=1).reshape(DFT_N1, SUBLANES, 2 * FOUR_WIDTH))
    t_ref[...] = jnp.concatenate(outs, axis=1).astype(BF16)


def _fourier_a_call(x, norm_in, w_fv_bf, chan, stage_a, tw):
    b, seq, _ = x.shape
    n1, n2 = DFT_N1, seq // DFT_N1
    step_rows = ROW_GROUPS * SUBLANES
    kron_rows = n1 * SUBLANES
    x4 = x.reshape(b, n1, n2, D_MODEL)
    return pl.pallas_call(
        _fourier_a_kernel,
        grid=(b, n2 // step_rows),
        in_specs=[pl.BlockSpec((None, n1, step_rows, D_MODEL), lambda i, j: (i, 0, j, 0)),
                  _const_spec((1, D_MODEL)),
                  _const_spec((D_MODEL, FOUR_WIDTH)),
                  _const_spec((FOUR_GDIM, 2 * FOUR_GDIM)),
                  _const_spec((2, kron_rows, 2 * kron_rows)),
                  pl.BlockSpec((ROW_GROUPS, kron_rows, 2 * FOUR_GDIM), lambda i, j: (j, 0, 0))],
        out_specs=pl.BlockSpec((None, n1, step_rows, 2 * FOUR_WIDTH), lambda i, j: (i, 0, j, 0)),
        out_shape=jax.ShapeDtypeStruct((b, n1, n2, 2 * FOUR_WIDTH), BF16),
        compiler_params=pltpu.CompilerParams(dimension_semantics=("arbitrary", "arbitrary")),
        name="fourier_a",
    )(x4, norm_in, w_fv_bf, chan, stage_a, tw)


def _fourier_b_kernel(t_ref, stb_ref, f_ref, *, k1_per_step, scale):
    for k in range(k1_per_step):
        t = t_ref[k]
        tcat = jnp.concatenate([t[:, :FOUR_WIDTH], t[:, FOUR_WIDTH:]], axis=0)
        f_ref[k] = (_dot(stb_ref[...], tcat) * scale).astype(BF16)


def _fourier_b_call(t, stage_b, seq):
    b, n1, n2, _ = t.shape
    k1_per_step = max(1, 1024 // n2)
    kern = functools.partial(_fourier_b_kernel, k1_per_step=k1_per_step,
                             scale=1.0 / math.sqrt(seq * FOUR_GDIM))
    return pl.pallas_call(
        kern,
        grid=(b, n1 // k1_per_step),
        in_specs=[pl.BlockSpec((None, k1_per_step, n2, 2 * FOUR_WIDTH), lambda i, j: (i, j, 0, 0)),
                  _const_spec((n2, 2 * n2))],
        out_specs=pl.BlockSpec((None, k1_per_step, n2, FOUR_WIDTH), lambda i, j: (i, j, 0, 0)),
        out_shape=jax.ShapeDtypeStruct((b, n1, n2, FOUR_WIDTH), BF16),
        compiler_params=pltpu.CompilerParams(dimension_semantics=("arbitrary", "arbitrary")),
        name="fourier_b",
    )(t, stage_b)


def _main_kernel(xm_ref, xp_ref, xq_ref, f_ref, kv_ref, gin_ref, win_ref, wpg_ref, psc_ref, wfg_ref,
                 wpo_ref, wfo_ref, wao_ref, bg_ref, wo_ref, gf_ref, out_ref, uext_ref, *, seq, tile):
    i = pl.program_id(1)
    last = pl.num_programs(1) - 1
    gin = gin_ref[...]
    x = xm_ref[...]
    xn = _rmsnorm(x, gin).astype(BF16)

    def proj(off, width, lhs=xn):
        return _dot(lhs, win_ref[:, off:off + width])

    def gate(branch):
        off = branch * D_MODEL
        return _sigmoid(proj(OFF_MG + off, D_MODEL) + bg_ref[:, off:off + D_MODEL])

    halo = jnp.concatenate([xp_ref[...], xq_ref[...]], axis=0)
    pvh = proj(OFF_PV, D_MODEL, _rmsnorm(halo, gin).astype(BF16))
    pv = proj(OFF_PV, D_MODEL)
    pv_prev = jnp.where(i > 0, pvh[:POOL_HALO], 0.0)
    pv_next = jnp.where(i < last, pvh[POOL_HALO:], 0.0)
    for s in range(D_MODEL // LANES):
        lanes = slice(s * LANES, (s + 1) * LANES)
        uext_ref[s, 0:POOL_HALO, :] = pv_prev[:, lanes]
        uext_ref[s, POOL_HALO:POOL_HALO + tile, :] = pv[:, lanes]
        uext_ref[s, POOL_HALO + tile:, :] = pv_next[:, lanes]
    t = i * tile + lax.broadcasted_iota(jnp.int32, (tile, 1), 0)
    slabs_per_group = POOL_GDIM // LANES
    ys = []
    for g, w in enumerate(POOL_WINDOWS):
        half = w // 2
        inv_cnt = 1.0 / (jnp.minimum(t + half, seq) - jnp.maximum(t - half, 0)).astype(F32)
        ps = []
        for s in range(g * slabs_per_group, (g + 1) * slabs_per_group):
            win = uext_ref[s, pl.ds(POOL_HALO - half, tile, stride=1), :]
            for jj in range(1, w):
                win = win + uext_ref[s, pl.ds(POOL_HALO - half + jj, tile, stride=1), :]
            ps.append(win * inv_cnt - pv[:, s * LANES:(s + 1) * LANES])
        p = jnp.concatenate(ps, axis=1).astype(BF16)
        ys.append(_dot(p, wpg_ref[g]))
    y = jnp.concatenate(ys, axis=1) * psc_ref[...]
    ya = _dot((y * _silu(proj(OFF_PG, D_MODEL))).astype(BF16), wpo_ref[...])
    merged = gate(0) * ya

    f = f_ref[...]
    yb = jnp.concatenate([_dot(f[:, h * FOUR_GDIM:(h + 1) * FOUR_GDIM], wfg_ref[h])
                          for h in range(FOUR_GROUPS)], axis=1)
    yb = _dot((yb * _silu(proj(OFF_FG, FOUR_WIDTH))).astype(BF16), wfo_ref[...])
    merged = merged + gate(1) * yb

    q = proj(OFF_Q, ATTN_WIDTH).astype(BF16)
    os_ = []
    for h in range(ATTN_HEADS):
        kh = kv_ref[:, h * HEAD_DIM:(h + 1) * HEAD_DIM]
        vh = kv_ref[:, ATTN_WIDTH + h * HEAD_DIM:ATTN_WIDTH + (h + 1) * HEAD_DIM]
        s = lax.dot_general(q[:, h * HEAD_DIM:(h + 1) * HEAD_DIM], kh, (((1,), (1,)), ((), ())),
                            preferred_element_type=F32) * (1.0 / math.sqrt(HEAD_DIM))
        e = jnp.exp(s - jnp.max(s, axis=-1, keepdims=True))
        l = jnp.sum(e, axis=-1, keepdims=True)
        os_.append(_dot(e.astype(BF16), vh) / l)
    o = jnp.concatenate(os_, axis=1)
    yc = _dot((o * _silu(proj(OFF_AG, ATTN_WIDTH))).astype(BF16), wao_ref[...])
    merged = merged + gate(2) * yc

    hres = x + _dot(merged.astype(BF16), wo_ref[...])
    out_ref[...] = _rmsnorm(hres, gf_ref[...])


def _main_call(x, f, kv, norm_in, w_in_bf, w_pool_grp_bf, pool_scale, w_four_grp_bf,
               w_pool_out_bf, w_four_out_bf, w_attn_out_bf, b_gate, w_o_bf, norm_f):
    b, seq, _ = x.shape
    tile = TOKEN_TILE
    nt = seq // tile
    hb = tile // POOL_HALO
    nhb = seq // POOL_HALO
    kern = functools.partial(_main_kernel, seq=seq, tile=tile)
    return pl.pallas_call(
        kern,
        grid=(b, nt),
        in_specs=[pl.BlockSpec((None, tile, D_MODEL), lambda bi, i: (bi, i, 0)),
                  pl.BlockSpec((None, POOL_HALO, D_MODEL), lambda bi, i: (bi, jnp.maximum(i * hb - 1, 0), 0)),
                  pl.BlockSpec((None, POOL_HALO, D_MODEL),
                               lambda bi, i: (bi, jnp.minimum((i + 1) * hb, nhb - 1), 0)),
                  pl.BlockSpec((None, tile, FOUR_WIDTH), lambda bi, i: (bi, i, 0)),
                  pl.BlockSpec((None, N_MEM, 2 * ATTN_WIDTH), lambda bi, i: (bi, 0, 0)),
                  _const_spec((1, D_MODEL)),
                  _const_spec(w_in_bf.shape),
                  _const_spec(w_pool_grp_bf.shape),
                  _const_spec((1, D_MODEL)),
                  _const_spec(w_four_grp_bf.shape),
                  _const_spec(w_pool_out_bf.shape),
                  _const_spec(w_four_out_bf.shape),
                  _const_spec(w_attn_out_bf.shape),
                  _const_spec((1, 3 * D_MODEL)),
                  _const_spec(w_o_bf.shape),
                  _const_spec((1, D_MODEL))],
        out_specs=pl.BlockSpec((None, tile, D_MODEL), lambda bi, i: (bi, i, 0)),
        out_shape=jax.ShapeDtypeStruct((b, seq, D_MODEL), F32),
        scratch_shapes=[pltpu.VMEM((D_MODEL // LANES, tile + 2 * POOL_HALO, LANES), F32)],
        compiler_params=pltpu.CompilerParams(dimension_semantics=("arbitrary", "arbitrary"),
                                             vmem_limit_bytes=VMEM_LIMIT),
        name="encoder_main",
    )(x, x, x, f, kv, norm_in, w_in_bf, w_pool_grp_bf, pool_scale, w_four_grp_bf,
      w_pool_out_bf, w_four_out_bf, w_attn_out_bf, b_gate, w_o_bf, norm_f)


def _trunk(x, mem, norm_in, norm_mem, w_in_bf, w_fv_bf, w_pool_grp_bf, pool_scale, w_four_grp_bf, w_kv_bf,
           w_pool_out_bf, w_four_out_bf, w_attn_out_bf, b_gate, w_o_bf, norm_f):
    b, seq, _ = x.shape
    chan, stage_a, tw, stage_b = _dft_constants(seq)
    chan, stage_a, stage_b = (jnp.asarray(a).astype(BF16) for a in (chan, stage_a, stage_b))
    kv = _kv_call(mem, norm_mem, w_kv_bf)
    t = _fourier_a_call(x, norm_in, w_fv_bf, chan, stage_a, tw)
    f = _fourier_b_call(t, stage_b, seq)
    f = jnp.swapaxes(f, 1, 2).reshape(b, seq, FOUR_WIDTH)
    return _main_call(x, f, kv, norm_in, w_in_bf, w_pool_grp_bf, pool_scale, w_four_grp_bf,
                      w_pool_out_bf, w_four_out_bf, w_attn_out_bf, b_gate, w_o_bf, norm_f)


def kernel(x_prompt, x_sample, mem_prompt, mem_sample, norm_in, norm_mem, w_in, w_pool_grp, pool_scale,
           w_four_grp, w_kv, w_pool_out, w_four_out, w_attn_out, b_gate, w_o, norm_f):
    assert norm_in.shape[0] == 1, "single-layer trunk"
    w_in_bf = w_in[0].astype(BF16)
    shared = (norm_in, norm_mem, w_in_bf, w_in_bf[:, OFF_FV:OFF_FV + FOUR_WIDTH], w_pool_grp[0].astype(BF16),
              pool_scale, w_four_grp[0].astype(BF16), w_kv[0].astype(BF16), w_pool_out[0].astype(BF16),
              w_four_out[0].astype(BF16), w_attn_out[0].astype(BF16), b_gate, w_o[0].astype(BF16),
              norm_f.reshape(1, D_MODEL))
    return (_trunk(x_prompt, mem_prompt, *shared), _trunk(x_sample, mem_sample, *shared))
```

```python
import functools
import math

import jax
import jax.numpy as jnp
import numpy as np
from jax import lax
from jax.experimental import pallas as pl
from jax.experimental.pallas import tpu as pltpu

D_MODEL = 1024
EPS = 1e-6
POOL_WINDOWS = (2, 4, 8, 16)
POOL_GDIM = 256
POOL_HALO = 8
FOUR_GROUPS = 4
FOUR_GDIM = 128
FOUR_WIDTH = FOUR_GROUPS * FOUR_GDIM
ATTN_HEADS = 4
HEAD_DIM = 128
ATTN_WIDTH = ATTN_HEADS * HEAD_DIM
N_MEM = 256
OFF_PV, OFF_PG, OFF_FV, OFF_FG, OFF_Q, OFF_AG, OFF_MG = 0, 1024, 2048, 2560, 3072, 3584, 4096

SUBLANES = 8
LANES = 128
DFT_N1 = 32
ROW_GROUPS = 2
TOKEN_TILE = 512
VMEM_LIMIT = 56 * 1024 * 1024

BF16 = jnp.bfloat16
F32 = jnp.float32


def _dot(a, b):
    return jnp.dot(a, b, preferred_element_type=F32)


def _pack_rows(w):
    *lead, k, n = w.shape
    wb = w.astype(BF16).reshape(*lead, k // 2, 2, n)
    return lax.bitcast_convert_type(jnp.swapaxes(wb, -1, -2), jnp.uint32)


def _unpack_rows(w_u32):
    return pltpu.bitcast(w_u32, BF16)


def _rmsnorm(x, g):
    r = lax.rsqrt(jnp.mean(x * x, axis=-1, keepdims=True) + EPS)
    return x * r * g


def _sigmoid(x):
    return 0.5 * jnp.tanh(0.5 * x) + 0.5


def _silu(x):
    return x * _sigmoid(x)


def _const_spec(shape):
    nd = len(shape)
    return pl.BlockSpec(shape, lambda *_: (0,) * nd, pipeline_mode=pl.Buffered(1))


@functools.cache
def _dft_constants(seq):
    n1 = DFT_N1
    n2 = seq // n1
    c = np.arange(FOUR_GDIM)
    ang = 2.0 * np.pi * np.outer(c, c) / FOUR_GDIM
    chan = np.concatenate([np.cos(ang), -np.sin(ang)], axis=1)
    k1 = np.arange(n1)
    psi = 2.0 * np.pi * np.outer(k1, k1) / n1
    eye = np.eye(SUBLANES)
    mr = np.kron(np.cos(psi), eye)
    mi = np.kron(-np.sin(psi), eye)
    stage_a = np.stack([np.concatenate([mr, -mi], axis=1),
                        np.concatenate([mi, mr], axis=1)], axis=0)
    j = np.arange(n2 // SUBLANES)
    r = np.arange(SUBLANES)
    s2 = (SUBLANES * j[:, None, None] + r[None, None, :])
    phi = 2.0 * np.pi * k1[None, :, None] * s2 / seq
    phi = phi.reshape(len(j), n1 * SUBLANES, 1)
    tw = np.concatenate([np.broadcast_to(np.cos(phi), phi.shape[:2] + (128,)),
                         np.broadcast_to(-np.sin(phi), phi.shape[:2] + (128,))], axis=2)
    k2 = np.arange(n2)
    th = 2.0 * np.pi * np.outer(k2, k2) / n2
    stage_b = np.concatenate([np.cos(th), np.sin(th)], axis=1)
    return tuple(np.asarray(a, np.float32) for a in (chan, stage_a, tw, stage_b))


def _kv_kernel(mem_ref, g_ref, wkv_ref, kv_ref):
    mn = _rmsnorm(mem_ref[...], g_ref[...]).astype(BF16)
    kv_ref[...] = _dot(mn, wkv_ref[...]).astype(BF16)


def _kv_call(mem, norm_mem, w_kv_bf):
    b = mem.shape[0]
    return pl.pallas_call(
        _kv_kernel,
        grid=(b,),
        in_specs=[pl.BlockSpec((None, N_MEM, D_MODEL), lambda i: (i, 0, 0)),
                  _const_spec((1, D_MODEL)),
                  _const_spec((D_MODEL, 2 * ATTN_WIDTH))],
        out_specs=pl.BlockSpec((None, N_MEM, 2 * ATTN_WIDTH), lambda i: (i, 0, 0)),
        out_shape=jax.ShapeDtypeStruct((b, N_MEM, 2 * ATTN_WIDTH), BF16),
        compiler_params=pltpu.CompilerParams(dimension_semantics=("arbitrary",)),
        name="kv_proj",
    )(mem, norm_mem, w_kv_bf)


def _fourier_a_kernel(x_ref, g_ref, wfv_ref, chan_ref, sta_ref, tw_ref, t_ref):
    step_rows = ROW_GROUPS * SUBLANES
    x = x_ref[...].reshape(DFT_N1 * step_rows, D_MODEL)
    xn = _rmsnorm(x, g_ref[...]).astype(BF16)
    fv = _dot(xn, wfv_ref[...]).astype(BF16)
    zr, zi = [], []
    for h in range(FOUR_GROUPS):
        z = _dot(fv[:, h * FOUR_GDIM:(h + 1) * FOUR_GDIM], chan_ref[...])
        zr.append(z[:, :FOUR_GDIM])
        zi.append(z[:, FOUR_GDIM:])
    zr = jnp.concatenate(zr, axis=1).reshape(DFT_N1, step_rows, FOUR_WIDTH)
    zi = jnp.concatenate(zi, axis=1).reshape(DFT_N1, step_rows, FOUR_WIDTH)
    kron_rows = DFT_N1 * SUBLANES
    outs = []
    for g in range(ROW_GROUPS):
        rs = slice(g * SUBLANES, (g + 1) * SUBLANES)
        zcat = jnp.concatenate([zr[:, rs, :].reshape(kron_rows, FOUR_WIDTH),
                                zi[:, rs, :].reshape(kron_rows, FOUR_WIDTH)], axis=0).astype(BF16)
        tr = _dot(sta_ref[0], zcat)
        ti = _dot(sta_ref[1], zcat)
        twr = tw_ref[g, :, :FOUR_GDIM]
        twi = tw_ref[g, :, FOUR_GDIM:]
        re, im = [], []
        for h in range(FOUR_GROUPS):
            a, b = tr[:, h * FOUR_GDIM:(h + 1) * FOUR_GDIM], ti[:, h * FOUR_GDIM:(h + 1) * FOUR_GDIM]
            re.append(a * twr - b * twi)
            im.append(a * twi + b * twr)
        outs.append(jnp.concatenate(re + im, axis=1).reshape(DFT_N1, SUBLANES, 2 * FOUR_WIDTH))
    t_ref[...] = jnp.concatenate(outs, axis=1).astype(BF16)


def _fourier_a_call(x, norm_in, w_fv_bf, chan, stage_a, tw):
    b, seq, _ = x.shape
    n1, n2 = DFT_N1, seq // DFT_N1
    step_rows = ROW_GROUPS * SUBLANES
    kron_rows = n1 * SUBLANES
    x4 = x.reshape(b, n1, n2, D_MODEL)
    return pl.pallas_call(
        _fourier_a_kernel,
        grid=(b, n2 // step_rows),
        in_specs=[pl.BlockSpec((None, n1, step_rows, D_MODEL), lambda i, j: (i, 0, j, 0)),
                  _const_spec((1, D_MODEL)),
                  _const_spec((D_MODEL, FOUR_WIDTH)),
                  _const_spec((FOUR_GDIM, 2 * FOUR_GDIM)),
                  _const_spec((2, kron_rows, 2 * kron_rows)),
                  pl.BlockSpec((ROW_GROUPS, kron_rows, 2 * FOUR_GDIM), lambda i, j: (j, 0, 0))],
        out_specs=pl.BlockSpec((None, n1, step_rows, 2 * FOUR_WIDTH), lambda i, j: (i, 0, j, 0)),
        out_shape=jax.ShapeDtypeStruct((b, n1, n2, 2 * FOUR_WIDTH), BF16),
        compiler_params=pltpu.CompilerParams(dimension_semantics=("arbitrary", "arbitrary")),
        name="fourier_a",
    )(x4, norm_in, w_fv_bf, chan, stage_a, tw)


def _fourier_b_kernel(t_ref, stb_ref, f_ref, *, k1_per_step, scale):
    for k in range(k1_per_step):
        t = t_ref[k]
        tcat = jnp.concatenate([t[:, :FOUR_WIDTH], t[:, FOUR_WIDTH:]], axis=0)
        f_ref[k] = (_dot(stb_ref[...], tcat) * scale).astype(BF16)


def _fourier_b_call(t, stage_b, seq):
    b, n1, n2, _ = t.shape
    k1_per_step = max(1, 1024 // n2)
    kern = functools.partial(_fourier_b_kernel, k1_per_step=k1_per_step,
                             scale=1.0 / math.sqrt(seq * FOUR_GDIM))
    return pl.pallas_call(
        kern,
        grid=(b, n1 // k1_per_step),
        in_specs=[pl.BlockSpec((None, k1_per_step, n2, 2 * FOUR_WIDTH), lambda i, j: (i, j, 0, 0)),
                  _const_spec((n2, 2 * n2))],
        out_specs=pl.BlockSpec((None, k1_per_step, n2, FOUR_WIDTH), lambda i, j: (i, j, 0, 0)),
        out_shape=jax.ShapeDtypeStruct((b, n1, n2, FOUR_WIDTH), BF16),
        compiler_params=pltpu.CompilerParams(dimension_semantics=("arbitrary", "arbitrary")),
        name="fourier_b",
    )(t, stage_b)


def _main_kernel(xm_ref, xp_ref, xq_ref, f_ref, kv_ref, gin_ref, win_ref, wpg_ref, psc_ref, wfg_ref,
                 wpo_ref, wfo_ref, wao_ref, bg_ref, wo_ref, gf_ref, out_ref, uext_ref, *, seq, tile):
    i = pl.program_id(1)
    last = pl.num_programs(1) - 1
    gin = gin_ref[...]
    x = xm_ref[...]
    xn = _rmsnorm(x, gin).astype(BF16)

    def proj(off, width, lhs=xn):
        return _dot(lhs, _unpack_rows(win_ref[:, off:off + width]))

    def gate(branch):
        off = branch * D_MODEL
        return _sigmoid(proj(OFF_MG + off, D_MODEL) + bg_ref[:, off:off + D_MODEL])


    halo = jnp.concatenate([xp_ref[...], xq_ref[...]], axis=0)
    pv_ext = proj(OFF_PV, D_MODEL, jnp.concatenate([xn, _rmsnorm(halo, gin).astype(BF16)], axis=0))
    pv = pv_ext[:tile]
    pv_prev = jnp.where(i > 0, pv_ext[tile:tile + POOL_HALO], 0.0)
    pv_next = jnp.where(i < last, pv_ext[tile + POOL_HALO:], 0.0)
    for s in range(D_MODEL // LANES):
        lanes = slice(s * LANES, (s + 1) * LANES)
        uext_ref[s, 0:POOL_HALO, :] = pv_prev[:, lanes]
        uext_ref[s, POOL_HALO:POOL_HALO + tile, :] = pv[:, lanes]
        uext_ref[s, POOL_HALO + tile:, :] = pv_next[:, lanes]

    q = proj(OFF_Q, ATTN_WIDTH).astype(BF16)
    silu_pg = _silu(proj(OFF_PG, D_MODEL))
    gate_a = gate(0)

    t = i * tile + lax.broadcasted_iota(jnp.int32, (tile, 1), 0)
    slabs_per_group = POOL_GDIM // LANES
    ys = []
    for g, w in enumerate(POOL_WINDOWS):
        half = w // 2
        inv_cnt = 1.0 / (jnp.minimum(t + half, seq) - jnp.maximum(t - half, 0)).astype(F32)
        ps = []
        for s in range(g * slabs_per_group, (g + 1) * slabs_per_group):
            win = uext_ref[s, pl.ds(POOL_HALO - half, tile, stride=1), :]
            for jj in range(1, w):
                win = win + uext_ref[s, pl.ds(POOL_HALO - half + jj, tile, stride=1), :]
            ps.append(win * inv_cnt - pv[:, s * LANES:(s + 1) * LANES])
        p = jnp.concatenate(ps, axis=1).astype(BF16)
        ys.append(_dot(p, _unpack_rows(wpg_ref[g])))
    ya_in = (jnp.concatenate(ys, axis=1) * psc_ref[...] * silu_pg).astype(BF16)

    es, ls = [], []
    for h in range(ATTN_HEADS):
        kh = kv_ref[:, h * HEAD_DIM:(h + 1) * HEAD_DIM]
        s = lax.dot_general(q[:, h * HEAD_DIM:(h + 1) * HEAD_DIM], kh, (((1,), (1,)), ((), ())),
                            preferred_element_type=F32) * (1.0 / math.sqrt(HEAD_DIM))
        e = jnp.exp(s - jnp.max(s, axis=-1, keepdims=True))
        ls.append(jnp.sum(e, axis=-1, keepdims=True))
        es.append(e.astype(BF16))

    silu_fg = _silu(proj(OFF_FG, FOUR_WIDTH))
    f = f_ref[...]
    yb_in = jnp.concatenate([_dot(f[:, h * FOUR_GDIM:(h + 1) * FOUR_GDIM], _unpack_rows(wfg_ref[h]))
                             for h in range(FOUR_GROUPS)], axis=1)
    yb_in = (yb_in * silu_fg).astype(BF16)
    gate_b = gate(1)

    ya = _dot(ya_in, _unpack_rows(wpo_ref[...]))
    merged = gate_a * ya

    silu_ag = _silu(proj(OFF_AG, ATTN_WIDTH))
    o = jnp.concatenate(
        [_dot(es[h], kv_ref[:, ATTN_WIDTH + h * HEAD_DIM:ATTN_WIDTH + (h + 1) * HEAD_DIM]) / ls[h]
         for h in range(ATTN_HEADS)], axis=1)
    yc_in = (o * silu_ag).astype(BF16)

    yb = _dot(yb_in, _unpack_rows(wfo_ref[...]))
    merged = merged + gate_b * yb
    gate_c = gate(2)
    yc = _dot(yc_in, _unpack_rows(wao_ref[...]))
    merged = merged + gate_c * yc

    hres = x + _dot(merged.astype(BF16), _unpack_rows(wo_ref[...]))
    out_ref[...] = _rmsnorm(hres, gf_ref[...])


def _main_call(x, f, kv, norm_in, w_in_pk, w_pool_grp_pk, pool_scale, w_four_grp_pk,
               w_pool_out_pk, w_four_out_pk, w_attn_out_pk, b_gate, w_o_pk, norm_f):
    b, seq, _ = x.shape
    tile = TOKEN_TILE
    nt = seq // tile
    hb = tile // POOL_HALO
    nhb = seq // POOL_HALO
    kern = functools.partial(_main_kernel, seq=seq, tile=tile)
    return pl.pallas_call(
        kern,
        grid=(b, nt),
        in_specs=[pl.BlockSpec((None, tile, D_MODEL), lambda bi, i: (bi, i, 0)),
                  pl.BlockSpec((None, POOL_HALO, D_MODEL), lambda bi, i: (bi, jnp.maximum(i * hb - 1, 0), 0)),
                  pl.BlockSpec((None, POOL_HALO, D_MODEL),
                               lambda bi, i: (bi, jnp.minimum((i + 1) * hb, nhb - 1), 0)),
                  pl.BlockSpec((None, tile, FOUR_WIDTH), lambda bi, i: (bi, i, 0)),
                  pl.BlockSpec((None, N_MEM, 2 * ATTN_WIDTH), lambda bi, i: (bi, 0, 0)),
                  _const_spec((1, D_MODEL)),
                  _const_spec(w_in_pk.shape),
                  _const_spec(w_pool_grp_pk.shape),
                  _const_spec((1, D_MODEL)),
                  _const_spec(w_four_grp_pk.shape),
                  _const_spec(w_pool_out_pk.shape),
                  _const_spec(w_four_out_pk.shape),
                  _const_spec(w_attn_out_pk.shape),
                  _const_spec((1, 3 * D_MODEL)),
                  _const_spec(w_o_pk.shape),
                  _const_spec((1, D_MODEL))],
        out_specs=pl.BlockSpec((None, tile, D_MODEL), lambda bi, i: (bi, i, 0)),
        out_shape=jax.ShapeDtypeStruct((b, seq, D_MODEL), F32),
        scratch_shapes=[pltpu.VMEM((D_MODEL // LANES, tile + 2 * POOL_HALO, LANES), F32)],
        compiler_params=pltpu.CompilerParams(dimension_semantics=("arbitrary", "arbitrary"),
                                             vmem_limit_bytes=VMEM_LIMIT),
        name="encoder_main",
    )(x, x, x, f, kv, norm_in, w_in_pk, w_pool_grp_pk, pool_scale, w_four_grp_pk,
      w_pool_out_pk, w_four_out_pk, w_attn_out_pk, b_gate, w_o_pk, norm_f)


def _trunk(x, mem, norm_in, norm_mem, w_in_pk, w_fv_bf, w_pool_grp_pk, pool_scale, w_four_grp_pk, w_kv_bf,
           w_pool_out_pk, w_four_out_pk, w_attn_out_pk, b_gate, w_o_pk, norm_f):
    b, seq, _ = x.shape
    chan, stage_a, tw, stage_b = _dft_constants(seq)
    chan, stage_a, stage_b = (jnp.asarray(a).astype(BF16) for a in (chan, stage_a, stage_b))
    kv = _kv_call(mem, norm_mem, w_kv_bf)
    t = _fourier_a_call(x, norm_in, w_fv_bf, chan, stage_a, tw)
    f = _fourier_b_call(t, stage_b, seq)
    f = jnp.swapaxes(f, 1, 2).reshape(b, seq, FOUR_WIDTH)
    return _main_call(x, f, kv, norm_in, w_in_pk, w_pool_grp_pk, pool_scale, w_four_grp_pk,
                      w_pool_out_pk, w_four_out_pk, w_attn_out_pk, b_gate, w_o_pk, norm_f)


def kernel(x_prompt, x_sample, mem_prompt, mem_sample, norm_in, norm_mem, w_in, w_pool_grp, pool_scale,
           w_four_grp, w_kv, w_pool_out, w_four_out, w_attn_out, b_gate, w_o, norm_f):
    assert norm_in.shape[0] == 1, "single-layer trunk"
    shared = (norm_in, norm_mem, _pack_rows(w_in[0]), w_in[0, :, OFF_FV:OFF_FV + FOUR_WIDTH].astype(BF16),
              _pack_rows(w_pool_grp[0]), pool_scale, _pack_rows(w_four_grp[0]), w_kv[0].astype(BF16),
              _pack_rows(w_pool_out[0]), _pack_rows(w_four_out[0]), _pack_rows(w_attn_out[0]), b_gate,
              _pack_rows(w_o[0]), norm_f.reshape(1, D_MODEL))
    return (_trunk(x_prompt, mem_prompt, *shared), _trunk(x_sample, mem_sample, *shared))
```

```python
import functools
import math

import jax
import jax.numpy as jnp
import numpy as np
from jax import lax
from jax.experimental import pallas as pl
from jax.experimental.pallas import tpu as pltpu

D_MODEL = 1024
EPS = 1e-6
POOL_WINDOWS = (2, 4, 8, 16)
POOL_GDIM = 256
POOL_HALO = 8
FOUR_GROUPS = 4
FOUR_GDIM = 128
FOUR_WIDTH = FOUR_GROUPS * FOUR_GDIM
ATTN_HEADS = 4
HEAD_DIM = 128
ATTN_WIDTH = ATTN_HEADS * HEAD_DIM
N_MEM = 256
OFF_PV, OFF_PG, OFF_FV, OFF_FG, OFF_Q, OFF_AG, OFF_MG = 0, 1024, 2048, 2560, 3072, 3584, 4096

SUBLANES = 8
LANES = 128
DFT_N1 = 32
ROW_GROUPS = 2
TOKEN_TILE = 512
PACK_STEPS = 8
VMEM_LIMIT = 56 * 1024 * 1024

BF16 = jnp.bfloat16
F32 = jnp.float32


def _dot(a, b):
    return jnp.dot(a, b, preferred_element_type=F32)


def _pack_kernel(*refs):
    n = len(refs) // 2
    for w_ref, o_ref in zip(refs[:n], refs[n:]):
        o_ref[...] = pltpu.bitcast(w_ref[...].astype(BF16), jnp.uint32)


def _pack_rows(*ws):
    flat = [w.reshape(-1, w.shape[-1]) for w in ws]
    in_specs, out_specs, out_shape = [], [], []
    for w in flat:
        k, n = w.shape
        in_specs.append(pl.BlockSpec((k // PACK_STEPS, n), lambda i: (i, 0)))
        out_specs.append(pl.BlockSpec((k // (2 * PACK_STEPS), n), lambda i: (i, 0)))
        out_shape.append(jax.ShapeDtypeStruct((k // 2, n), jnp.uint32))
    packed = pl.pallas_call(
        _pack_kernel,
        grid=(PACK_STEPS,),
        in_specs=in_specs,
        out_specs=out_specs,
        out_shape=out_shape,
        compiler_params=pltpu.CompilerParams(dimension_semantics=("arbitrary",)),
        name="pack_weights",
    )(*flat)
    return [p.reshape(*w.shape[:-2], w.shape[-2] // 2, w.shape[-1]) for p, w in zip(packed, ws)]


def _unpack_rows(w_u32):
    return pltpu.bitcast(w_u32, BF16)


def _rmsnorm(x, g):
    r = lax.rsqrt(jnp.mean(x * x, axis=-1, keepdims=True) + EPS)
    return x * r * g


def _sigmoid(x):
    return 0.5 * jnp.tanh(0.5 * x) + 0.5


def _silu(x):
    return x * _sigmoid(x)


def _const_spec(shape):
    nd = len(shape)
    return pl.BlockSpec(shape, lambda *_: (0,) * nd, pipeline_mode=pl.Buffered(1))


@functools.cache
def _dft_constants(seq):
    n1 = DFT_N1
    n2 = seq // n1
    c = np.arange(FOUR_GDIM)
    ang = 2.0 * np.pi * np.outer(c, c) / FOUR_GDIM
    chan = np.concatenate([np.cos(ang), -np.sin(ang)], axis=1)
    k1 = np.arange(n1)
    psi = 2.0 * np.pi * np.outer(k1, k1) / n1
    eye = np.eye(SUBLANES)
    mr = np.kron(np.cos(psi), eye)
    mi = np.kron(-np.sin(psi), eye)
    stage_a = np.stack([np.concatenate([mr, -mi], axis=1),
                        np.concatenate([mi, mr], axis=1)], axis=0)
    j = np.arange(n2 // SUBLANES)
    r = np.arange(SUBLANES)
    s2 = (SUBLANES * j[:, None, None] + r[None, None, :])
    phi = 2.0 * np.pi * k1[None, :, None] * s2 / seq
    phi = phi.reshape(len(j), n1 * SUBLANES, 1)
    tw = np.concatenate([np.broadcast_to(np.cos(phi), phi.shape[:2] + (128,)),
                         np.broadcast_to(-np.sin(phi), phi.shape[:2] + (128,))], axis=2)
    k2 = np.arange(n2)
    th = 2.0 * np.pi * np.outer(k2, k2) / n2
    stage_b = np.concatenate([np.cos(th), np.sin(th)], axis=1)
    return tuple(np.asarray(a, np.float32) for a in (chan, stage_a, tw, stage_b))


def _kv_kernel(mem_ref, g_ref, wkv_ref, kv_ref):
    mn = _rmsnorm(mem_ref[...], g_ref[...]).astype(BF16)
    kv_ref[...] = _dot(mn, _unpack_rows(wkv_ref[...])).astype(BF16)


def _kv_call(mem, norm_mem, w_kv_pk):
    b = mem.shape[0]
    return pl.pallas_call(
        _kv_kernel,
        grid=(b,),
        in_specs=[pl.BlockSpec((None, N_MEM, D_MODEL), lambda i: (i, 0, 0)),
                  _const_spec((1, D_MODEL)),
                  _const_spec(w_kv_pk.shape)],
        out_specs=pl.BlockSpec((None, N_MEM, 2 * ATTN_WIDTH), lambda i: (i, 0, 0)),
        out_shape=jax.ShapeDtypeStruct((b, N_MEM, 2 * ATTN_WIDTH), BF16),
        compiler_params=pltpu.CompilerParams(dimension_semantics=("arbitrary",)),
        name="kv_proj",
    )(mem, norm_mem, w_kv_pk)


def _fourier_a_kernel(x_ref, g_ref, wfv_ref, chan_ref, sta_ref, tw_ref, t_ref):
    step_rows = ROW_GROUPS * SUBLANES
    x = x_ref[...].reshape(DFT_N1 * step_rows, D_MODEL)
    xn = _rmsnorm(x, g_ref[...]).astype(BF16)
    fv = _dot(xn, _unpack_rows(wfv_ref[...])).astype(BF16)
    zr, zi = [], []
    for h in range(FOUR_GROUPS):
        z = _dot(fv[:, h * FOUR_GDIM:(h + 1) * FOUR_GDIM], chan_ref[...])
        zr.append(z[:, :FOUR_GDIM])
        zi.append(z[:, FOUR_GDIM:])
    zr = jnp.concatenate(zr, axis=1).reshape(DFT_N1, step_rows, FOUR_WIDTH)
    zi = jnp.concatenate(zi, axis=1).reshape(DFT_N1, step_rows, FOUR_WIDTH)
    kron_rows = DFT_N1 * SUBLANES
    outs = []
    for g in range(ROW_GROUPS):
        rs = slice(g * SUBLANES, (g + 1) * SUBLANES)
        zcat = jnp.concatenate([zr[:, rs, :].reshape(kron_rows, FOUR_WIDTH),
                                zi[:, rs, :].reshape(kron_rows, FOUR_WIDTH)], axis=0).astype(BF16)
        tr = _dot(sta_ref[0], zcat)
        ti = _dot(sta_ref[1], zcat)
        twr = tw_ref[g, :, :FOUR_GDIM]
        twi = tw_ref[g, :, FOUR_GDIM:]
        re, im = [], []
        for h in range(FOUR_GROUPS):
            a, b = tr[:, h * FOUR_GDIM:(h + 1) * FOUR_GDIM], ti[:, h * FOUR_GDIM:(h + 1) * FOUR_GDIM]
            re.append(a * twr - b * twi)
            im.append(a * twi + b * twr)
        outs.append(jnp.concatenate(re + im, axis=1).reshape(DFT_N1, SUBLANES, 2 * FOUR_WIDTH))
    t_ref[...] = jnp.concatenate(outs, axis=1).astype(BF16)


def _fourier_a_call(x, norm_in, w_in_pk, chan, stage_a, tw):
    b, seq, _ = x.shape
    n1, n2 = DFT_N1, seq // DFT_N1
    step_rows = ROW_GROUPS * SUBLANES
    kron_rows = n1 * SUBLANES
    x4 = x.reshape(b, n1, n2, D_MODEL)
    return pl.pallas_call(
        _fourier_a_kernel,
        grid=(b, n2 // step_rows),
        in_specs=[pl.BlockSpec((None, n1, step_rows, D_MODEL), lambda i, j: (i, 0, j, 0)),
                  _const_spec((1, D_MODEL)),
                  pl.BlockSpec((D_MODEL // 2, FOUR_WIDTH), lambda i, j: (0, OFF_FV // FOUR_WIDTH),
                               pipeline_mode=pl.Buffered(1)),
                  _const_spec((FOUR_GDIM, 2 * FOUR_GDIM)),
                  _const_spec((2, kron_rows, 2 * kron_rows)),
                  pl.BlockSpec((ROW_GROUPS, kron_rows, 2 * FOUR_GDIM), lambda i, j: (j, 0, 0))],
        out_specs=pl.BlockSpec((None, n1, step_rows, 2 * FOUR_WIDTH), lambda i, j: (i, 0, j, 0)),
        out_shape=jax.ShapeDtypeStruct((b, n1, n2, 2 * FOUR_WIDTH), BF16),
        compiler_params=pltpu.CompilerParams(dimension_semantics=("arbitrary", "arbitrary")),
        name="fourier_a",
    )(x4, norm_in, w_in_pk, chan, stage_a, tw)


def _fourier_b_kernel(t_ref, stb_ref, f_ref, *, k1_per_step, scale):
    for k in range(k1_per_step):
        t = t_ref[k]
        tcat = jnp.concatenate([t[:, :FOUR_WIDTH], t[:, FOUR_WIDTH:]], axis=0)
        f_ref[k] = (_dot(stb_ref[...], tcat) * scale).astype(BF16)


def _fourier_b_call(t, stage_b, seq):
    b, n1, n2, _ = t.shape
    k1_per_step = max(1, 1024 // n2)
    kern = functools.partial(_fourier_b_kernel, k1_per_step=k1_per_step,
                             scale=1.0 / math.sqrt(seq * FOUR_GDIM))
    return pl.pallas_call(
        kern,
        grid=(b, n1 // k1_per_step),
        in_specs=[pl.BlockSpec((None, k1_per_step, n2, 2 * FOUR_WIDTH), lambda i, j: (i, j, 0, 0)),
                  _const_spec((n2, 2 * n2))],
        out_specs=pl.BlockSpec((None, k1_per_step, n2, FOUR_WIDTH), lambda i, j: (i, j, 0, 0)),
        out_shape=jax.ShapeDtypeStruct((b, n1, n2, FOUR_WIDTH), BF16),
        compiler_params=pltpu.CompilerParams(dimension_semantics=("arbitrary", "arbitrary")),
        name="fourier_b",
    )(t, stage_b)


def _main_kernel(xm_ref, xp_ref, xq_ref, f_ref, kv_ref, gin_ref, win_ref, wpg_ref, psc_ref, wfg_ref,
                 wpo_ref, wfo_ref, wao_ref, bg_ref, wo_ref, gf_ref, out_ref, uext_ref, *, seq, tile):
    i = pl.program_id(1)
    last = pl.num_programs(1) - 1
    gin = gin_ref[...]
    x = xm_ref[...]
    xn = _rmsnorm(x, gin).astype(BF16)

    def proj(off, width, lhs=xn):
        return _dot(lhs, _unpack_rows(win_ref[:, off:off + width]))

    def gate(branch):
        off = branch * D_MODEL
        return _sigmoid(proj(OFF_MG + off, D_MODEL) + bg_ref[:, off:off + D_MODEL])


    halo = jnp.concatenate([xp_ref[...], xq_ref[...]], axis=0)
    pv_ext = proj(OFF_PV, D_MODEL, jnp.concatenate([xn, _rmsnorm(halo, gin).astype(BF16)], axis=0))
    pv = pv_ext[:tile]
    pv_prev = jnp.where(i > 0, pv_ext[tile:tile + POOL_HALO], 0.0)
    pv_next = jnp.where(i < last, pv_ext[tile + POOL_HALO:], 0.0)
    for s in range(D_MODEL // LANES):
        lanes = slice(s * LANES, (s + 1) * LANES)
        uext_ref[s, 0:POOL_HALO, :] = pv_prev[:, lanes]
        uext_ref[s, POOL_HALO:POOL_HALO + tile, :] = pv[:, lanes]
        uext_ref[s, POOL_HALO + tile:, :] = pv_next[:, lanes]

    q = proj(OFF_Q, ATTN_WIDTH).astype(BF16)
    silu_pg = _silu(proj(OFF_PG, D_MODEL))
    gate_a = gate(0)

    t = i * tile + lax.broadcasted_iota(jnp.int32, (tile, 1), 0)
    slabs_per_group = POOL_GDIM // LANES
    ys = []
    for g, w in enumerate(POOL_WINDOWS):
        half = w // 2
        inv_cnt = 1.0 / (jnp.minimum(t + half, seq) - jnp.maximum(t - half, 0)).astype(F32)
        ps = []
        for s in range(g * slabs_per_group, (g + 1) * slabs_per_group):
            win = uext_ref[s, pl.ds(POOL_HALO - half, tile, stride=1), :]
            for jj in range(1, w):
                win = win + uext_ref[s, pl.ds(POOL_HALO - half + jj, tile, stride=1), :]
            ps.append(win * inv_cnt - pv[:, s * LANES:(s + 1) * LANES])
        p = jnp.concatenate(ps, axis=1).astype(BF16)
        ys.append(_dot(p, _unpack_rows(wpg_ref[g])))
    ya_in = (jnp.concatenate(ys, axis=1) * psc_ref[...] * silu_pg).astype(BF16)

    es, ls = [], []
    for h in range(ATTN_HEADS):
        kh = kv_ref[:, h * HEAD_DIM:(h + 1) * HEAD_DIM]
        s = lax.dot_general(q[:, h * HEAD_DIM:(h + 1) * HEAD_DIM], kh, (((1,), (1,)), ((), ())),
                            preferred_element_type=F32) * (1.0 / math.sqrt(HEAD_DIM))
        e = jnp.exp(s - jnp.max(s, axis=-1, keepdims=True))
        ls.append(jnp.sum(e, axis=-1, keepdims=True))
        es.append(e.astype(BF16))

    silu_fg = _silu(proj(OFF_FG, FOUR_WIDTH))
    f = f_ref[...]
    yb_in = jnp.concatenate([_dot(f[:, h * FOUR_GDIM:(h + 1) * FOUR_GDIM], _unpack_rows(wfg_ref[h]))
                             for h in range(FOUR_GROUPS)], axis=1)
    yb_in = (yb_in * silu_fg).astype(BF16)
    gate_b = gate(1)

    ya = _dot(ya_in, _unpack_rows(wpo_ref[...]))
    merged = gate_a * ya

    silu_ag = _silu(proj(OFF_AG, ATTN_WIDTH))
    o = jnp.concatenate(
        [_dot(es[h], kv_ref[:, ATTN_WIDTH + h * HEAD_DIM:ATTN_WIDTH + (h + 1) * HEAD_DIM]) / ls[h]
         for h in range(ATTN_HEADS)], axis=1)
    yc_in = (o * silu_ag).astype(BF16)

    yb = _dot(yb_in, _unpack_rows(wfo_ref[...]))
    merged = merged + gate_b * yb
    gate_c = gate(2)
    yc = _dot(yc_in, _unpack_rows(wao_ref[...]))
    merged = merged + gate_c * yc

    hres = x + _dot(merged.astype(BF16), _unpack_rows(wo_ref[...]))
    out_ref[...] = _rmsnorm(hres, gf_ref[...])


def _main_call(x, f, kv, norm_in, w_in_pk, w_pool_grp_pk, pool_scale, w_four_grp_pk,
               w_pool_out_pk, w_four_out_pk, w_attn_out_pk, b_gate, w_o_pk, norm_f):
    b, seq, _ = x.shape
    tile = TOKEN_TILE
    nt = seq // tile
    hb = tile // POOL_HALO
    nhb = seq // POOL_HALO
    kern = functools.partial(_main_kernel, seq=seq, tile=tile)
    return pl.pallas_call(
        kern,
        grid=(b, nt),
        in_specs=[pl.BlockSpec((None, tile, D_MODEL), lambda bi, i: (bi, i, 0)),
                  pl.BlockSpec((None, POOL_HALO, D_MODEL), lambda bi, i: (bi, jnp.maximum(i * hb - 1, 0), 0)),
                  pl.BlockSpec((None, POOL_HALO, D_MODEL),
                               lambda bi, i: (bi, jnp.minimum((i + 1) * hb, nhb - 1), 0)),
                  pl.BlockSpec((None, tile, FOUR_WIDTH), lambda bi, i: (bi, i, 0)),
                  pl.BlockSpec((None, N_MEM, 2 * ATTN_WIDTH), lambda bi, i: (bi, 0, 0)),
                  _const_spec((1, D_MODEL)),
                  _const_spec(w_in_pk.shape),
                  _const_spec(w_pool_grp_pk.shape),
                  _const_spec((1, D_MODEL)),
                  _const_spec(w_four_grp_pk.shape),
                  _const_spec(w_pool_out_pk.shape),
                  _const_spec(w_four_out_pk.shape),
                  _const_spec(w_attn_out_pk.shape),
                  _const_spec((1, 3 * D_MODEL)),
                  _const_spec(w_o_pk.shape),
                  _const_spec((1, D_MODEL))],
        out_specs=pl.BlockSpec((None, tile, D_MODEL), lambda bi, i: (bi, i, 0)),
        out_shape=jax.ShapeDtypeStruct((b, seq, D_MODEL), F32),
        scratch_shapes=[pltpu.VMEM((D_MODEL // LANES, tile + 2 * POOL_HALO, LANES), F32)],
        compiler_params=pltpu.CompilerParams(dimension_semantics=("arbitrary", "arbitrary"),
                                             vmem_limit_bytes=VMEM_LIMIT),
        name="encoder_main",
    )(x, x, x, f, kv, norm_in, w_in_pk, w_pool_grp_pk, pool_scale, w_four_grp_pk,
      w_pool_out_pk, w_four_out_pk, w_attn_out_pk, b_gate, w_o_pk, norm_f)


def _trunk(x, mem, norm_in, norm_mem, w_in_pk, w_pool_grp_pk, pool_scale, w_four_grp_pk, w_kv_pk,
           w_pool_out_pk, w_four_out_pk, w_attn_out_pk, b_gate, w_o_pk, norm_f):
    b, seq, _ = x.shape
    chan, stage_a, tw, stage_b = _dft_constants(seq)
    chan, stage_a, stage_b = (jnp.asarray(a).astype(BF16) for a in (chan, stage_a, stage_b))
    kv = _kv_call(mem, norm_mem, w_kv_pk)
    t = _fourier_a_call(x, norm_in, w_in_pk, chan, stage_a, tw)
    f = _fourier_b_call(t, stage_b, seq)
    f = jnp.swapaxes(f, 1, 2).reshape(b, seq, FOUR_WIDTH)
    return _main_call(x, f, kv, norm_in, w_in_pk, w_pool_grp_pk, pool_scale, w_four_grp_pk,
                      w_pool_out_pk, w_four_out_pk, w_attn_out_pk, b_gate, w_o_pk, norm_f)


def kernel(x_prompt, x_sample, mem_prompt, mem_sample, norm_in, norm_mem, w_in, w_pool_grp, pool_scale,
           w_four_grp, w_kv, w_pool_out, w_four_out, w_attn_out, b_gate, w_o, norm_f):
    assert norm_in.shape[0] == 1, "single-layer trunk"
    w_in_pk, w_pool_grp_pk, w_four_grp_pk, w_kv_pk, w_pool_out_pk, w_four_out_pk, w_attn_out_pk, w_o_pk = _pack_rows(
        w_in[0], w_pool_grp[0], w_four_grp[0], w_kv[0], w_pool_out[0], w_four_out[0], w_attn_out[0], w_o[0])
    shared = (norm_in, norm_mem, w_in_pk, w_pool_grp_pk, pool_scale, w_four_grp_pk, w_kv_pk, w_pool_out_pk,
              w_four_out_pk, w_attn_out_pk, b_gate, w_o_pk, norm_f.reshape(1, D_MODEL))
    return (_trunk(x_prompt, mem_prompt, *shared), _trunk(x_sample, mem_sample, *shared))
```

```python
import functools
import math

import jax
import jax.numpy as jnp
import numpy as np
from jax import lax
from jax.experimental import pallas as pl
from jax.experimental.pallas import tpu as pltpu

D_MODEL = 1024
EPS = 1e-6
POOL_WINDOWS = (2, 4, 8, 16)
POOL_GDIM = 256
POOL_HALO = 8
FOUR_GROUPS = 4
FOUR_GDIM = 128
FOUR_WIDTH = FOUR_GROUPS * FOUR_GDIM
ATTN_HEADS = 4
HEAD_DIM = 128
ATTN_WIDTH = ATTN_HEADS * HEAD_DIM
N_MEM = 256
OFF_PV, OFF_PG, OFF_FV, OFF_FG, OFF_Q, OFF_AG, OFF_MG = 0, 1024, 2048, 2560, 3072, 3584, 4096

SUBLANES = 8
LANES = 128
DFT_N1 = 32
ROW_GROUPS = 4
TOKEN_TILE = 512
PACK_STEPS = 8
VMEM_LIMIT = 56 * 1024 * 1024

BF16 = jnp.bfloat16
F32 = jnp.float32


def _dot(a, b):
    return jnp.dot(a, b, preferred_element_type=F32)


def _pack_kernel(*refs):
    n = len(refs) // 2
    for w_ref, o_ref in zip(refs[:n], refs[n:]):
        o_ref[...] = pltpu.bitcast(w_ref[...].astype(BF16), jnp.uint32)


def _pack_rows(*ws):
    flat = [w.reshape(-1, w.shape[-1]) for w in ws]
    in_specs, out_specs, out_shape = [], [], []
    for w in flat:
        k, n = w.shape
        in_specs.append(pl.BlockSpec((k // PACK_STEPS, n), lambda i: (i, 0)))
        out_specs.append(pl.BlockSpec((k // (2 * PACK_STEPS), n), lambda i: (i, 0)))
        out_shape.append(jax.ShapeDtypeStruct((k // 2, n), jnp.uint32))
    packed = pl.pallas_call(
        _pack_kernel,
        grid=(PACK_STEPS,),
        in_specs=in_specs,
        out_specs=out_specs,
        out_shape=out_shape,
        compiler_params=pltpu.CompilerParams(dimension_semantics=("arbitrary",)),
        name="pack_weights",
    )(*flat)
    return [p.reshape(*w.shape[:-2], w.shape[-2] // 2, w.shape[-1]) for p, w in zip(packed, ws)]


def _unpack_rows(w_u32):
    return pltpu.bitcast(w_u32, BF16)


def _rmsnorm(x, g):
    r = lax.rsqrt(jnp.mean(x * x, axis=-1, keepdims=True) + EPS)
    return x * r * g


def _sigmoid(x):
    return 0.5 * jnp.tanh(0.5 * x) + 0.5


def _silu(x):
    return x * _sigmoid(x)


def _const_spec(shape):
    nd = len(shape)
    return pl.BlockSpec(shape, lambda *_: (0,) * nd, pipeline_mode=pl.Buffered(1))


@functools.cache
def _channel_dft():
    c = np.arange(FOUR_GDIM)
    ang = 2.0 * np.pi * np.outer(c, c) / FOUR_GDIM
    return np.concatenate([np.cos(ang), -np.sin(ang)], axis=0).astype(np.float32)


@functools.cache
def _dft_constants(seq):
    n1 = DFT_N1
    n2 = seq // n1
    k1 = np.arange(n1)
    psi = 2.0 * np.pi * np.outer(k1, k1) / n1
    eye = np.eye(SUBLANES)
    mr = np.kron(np.cos(psi), eye)
    mi = np.kron(-np.sin(psi), eye)
    stage_a = np.stack([np.concatenate([mr, -mi], axis=1),
                        np.concatenate([mi, mr], axis=1)], axis=0)
    j = np.arange(n2 // SUBLANES)
    r = np.arange(SUBLANES)
    s2 = (SUBLANES * j[:, None, None] + r[None, None, :])
    phi = 2.0 * np.pi * k1[None, :, None] * s2 / seq
    phi = phi.reshape(len(j), n1 * SUBLANES, 1)
    tw = np.concatenate([np.broadcast_to(np.cos(phi), phi.shape[:2] + (128,)),
                         np.broadcast_to(-np.sin(phi), phi.shape[:2] + (128,))], axis=2)
    k2 = np.arange(n2)
    th = 2.0 * np.pi * np.outer(k2, k2) / n2
    stage_b = np.concatenate([np.cos(th), np.sin(th)], axis=1)
    return tuple(np.asarray(a, np.float32) for a in (stage_a, tw, stage_b))


def _fold_chan_kernel(c_ref, wg_ref, o_ref):
    for h in range(FOUR_GROUPS):
        r = jnp.dot(c_ref[...], wg_ref[h], precision=lax.Precision.HIGHEST, preferred_element_type=F32)
        folded = jnp.concatenate([r[:FOUR_GDIM], r[FOUR_GDIM:]], axis=1)
        o_ref[h] = pltpu.bitcast(folded.astype(BF16), jnp.uint32)


def _fold_chan_call(w_four_grp):
    return pl.pallas_call(
        _fold_chan_kernel,
        out_shape=jax.ShapeDtypeStruct((FOUR_GROUPS, FOUR_GDIM // 2, 2 * FOUR_GDIM), jnp.uint32),
        name="fold_chan",
    )(jnp.asarray(_channel_dft()), w_four_grp)


def _kv_kernel(mem_ref, g_ref, wkv_ref, kv_ref):
    mn = _rmsnorm(mem_ref[...], g_ref[...]).astype(BF16)
    kv_ref[...] = _dot(mn, _unpack_rows(wkv_ref[...])).astype(BF16)


def _kv_call(mem, norm_mem, w_kv_pk):
    b = mem.shape[0]
    return pl.pallas_call(
        _kv_kernel,
        grid=(b,),
        in_specs=[pl.BlockSpec((None, N_MEM, D_MODEL), lambda i: (i, 0, 0)),
                  _const_spec((1, D_MODEL)),
                  _const_spec(w_kv_pk.shape)],
        out_specs=pl.BlockSpec((None, N_MEM, 2 * ATTN_WIDTH), lambda i: (i, 0, 0)),
        out_shape=jax.ShapeDtypeStruct((b, N_MEM, 2 * ATTN_WIDTH), BF16),
        compiler_params=pltpu.CompilerParams(dimension_semantics=("arbitrary",)),
        name="kv_proj",
    )(mem, norm_mem, w_kv_pk)


def _fourier_a_kernel(x_ref, g_ref, wfv_ref, chan_ref, sta_ref, tw_ref, t_ref):
    step_rows = ROW_GROUPS * SUBLANES
    x = x_ref[...].reshape(DFT_N1 * step_rows, D_MODEL)
    xn = _rmsnorm(x, g_ref[...]).astype(BF16)
    fv = _dot(xn, _unpack_rows(wfv_ref[...])).astype(BF16)
    zr, zi = [], []
    for h in range(FOUR_GROUPS):
        z = _dot(fv[:, h * FOUR_GDIM:(h + 1) * FOUR_GDIM], _unpack_rows(chan_ref[h]))
        zr.append(z[:, :FOUR_GDIM])
        zi.append(z[:, FOUR_GDIM:])
    zr = jnp.concatenate(zr, axis=1).reshape(DFT_N1, step_rows, FOUR_WIDTH)
    zi = jnp.concatenate(zi, axis=1).reshape(DFT_N1, step_rows, FOUR_WIDTH)
    kron_rows = DFT_N1 * SUBLANES
    outs = []
    for g in range(ROW_GROUPS):
        rs = slice(g * SUBLANES, (g + 1) * SUBLANES)
        zcat = jnp.concatenate([zr[:, rs, :].reshape(kron_rows, FOUR_WIDTH),
                                zi[:, rs, :].reshape(kron_rows, FOUR_WIDTH)], axis=0).astype(BF16)
        tr = _dot(sta_ref[0], zcat)
        ti = _dot(sta_ref[1], zcat)
        twr = tw_ref[g, :, :FOUR_GDIM]
        twi = tw_ref[g, :, FOUR_GDIM:]
        re, im = [], []
        for h in range(FOUR_GROUPS):
            a, b = tr[:, h * FOUR_GDIM:(h + 1) * FOUR_GDIM], ti[:, h * FOUR_GDIM:(h + 1) * FOUR_GDIM]
            re.append(a * twr - b * twi)
            im.append(a * twi + b * twr)
        outs.append(jnp.concatenate(re + im, axis=1).reshape(DFT_N1, SUBLANES, 2 * FOUR_WIDTH))
    t_ref[...] = jnp.concatenate(outs, axis=1).astype(BF16)


def _fourier_a_call(x, norm_in, w_in_pk, chan, stage_a, tw):
    b, seq, _ = x.shape
    n1, n2 = DFT_N1, seq // DFT_N1
    step_rows = ROW_GROUPS * SUBLANES
    kron_rows = n1 * SUBLANES
    x4 = x.reshape(b, n1, n2, D_MODEL)
    return pl.pallas_call(
        _fourier_a_kernel,
        grid=(b, n2 // step_rows),
        in_specs=[pl.BlockSpec((None, n1, step_rows, D_MODEL), lambda i, j: (i, 0, j, 0)),
                  _const_spec((1, D_MODEL)),
                  pl.BlockSpec((D_MODEL // 2, FOUR_WIDTH), lambda i, j: (0, OFF_FV // FOUR_WIDTH),
                               pipeline_mode=pl.Buffered(1)),
                  _const_spec(chan.shape),
                  _const_spec((2, kron_rows, 2 * kron_rows)),
                  pl.BlockSpec((ROW_GROUPS, kron_rows, 2 * FOUR_GDIM), lambda i, j: (j, 0, 0))],
        out_specs=pl.BlockSpec((None, n1, step_rows, 2 * FOUR_WIDTH), lambda i, j: (i, 0, j, 0)),
        out_shape=jax.ShapeDtypeStruct((b, n1, n2, 2 * FOUR_WIDTH), BF16),
        compiler_params=pltpu.CompilerParams(dimension_semantics=("arbitrary", "arbitrary")),
        name="fourier_a",
    )(x4, norm_in, w_in_pk, chan, stage_a, tw)


def _fourier_b_kernel(t_ref, stb_ref, f_ref, *, k1_per_step, scale):
    for k in range(k1_per_step):
        t = t_ref[k]
        tcat = jnp.concatenate([t[:, :FOUR_WIDTH], t[:, FOUR_WIDTH:]], axis=0)
        f_ref[k] = (_dot(stb_ref[...], tcat) * scale).astype(BF16)


def _fourier_b_call(t, stage_b, seq):
    b, n1, n2, _ = t.shape
    k1_per_step = max(1, 1024 // n2)
    kern = functools.partial(_fourier_b_kernel, k1_per_step=k1_per_step,
                             scale=1.0 / math.sqrt(seq * FOUR_GDIM))
    return pl.pallas_call(
        kern,
        grid=(b, n1 // k1_per_step),
        in_specs=[pl.BlockSpec((None, k1_per_step, n2, 2 * FOUR_WIDTH), lambda i, j: (i, j, 0, 0)),
                  _const_spec((n2, 2 * n2))],
        out_specs=pl.BlockSpec((None, k1_per_step, n2, FOUR_WIDTH), lambda i, j: (i, j, 0, 0)),
        out_shape=jax.ShapeDtypeStruct((b, n1, n2, FOUR_WIDTH), BF16),
        compiler_params=pltpu.CompilerParams(dimension_semantics=("arbitrary", "arbitrary")),
        name="fourier_b",
    )(t, stage_b)


def _main_kernel(xm_ref, xp_ref, xq_ref, f_ref, kv_ref, gin_ref, win_ref, wpg_ref, psc_ref,
                 wpo_ref, wfo_ref, wao_ref, bg_ref, wo_ref, gf_ref, out_ref, uext_ref, *, seq, tile):
    i = pl.program_id(1)
    last = pl.num_programs(1) - 1
    gin = gin_ref[...]
    x = xm_ref[...]
    xn = _rmsnorm(x, gin).astype(BF16)

    def proj(off, width, lhs=xn):
        return _dot(lhs, _unpack_rows(win_ref[:, off:off + width]))

    def gate(branch):
        off = branch * D_MODEL
        return _sigmoid(proj(OFF_MG + off, D_MODEL) + bg_ref[:, off:off + D_MODEL])


    halo = jnp.concatenate([xp_ref[...], xq_ref[...]], axis=0)
    pv_ext = proj(OFF_PV, D_MODEL, jnp.concatenate([xn, _rmsnorm(halo, gin).astype(BF16)], axis=0))
    pv = pv_ext[:tile]
    pv_prev = jnp.where(i > 0, pv_ext[tile:tile + POOL_HALO], 0.0)
    pv_next = jnp.where(i < last, pv_ext[tile + POOL_HALO:], 0.0)
    for s in range(D_MODEL // LANES):
        lanes = slice(s * LANES, (s + 1) * LANES)
        uext_ref[s, 0:POOL_HALO, :] = pv_prev[:, lanes]
        uext_ref[s, POOL_HALO:POOL_HALO + tile, :] = pv[:, lanes]
        uext_ref[s, POOL_HALO + tile:, :] = pv_next[:, lanes]

    q = proj(OFF_Q, ATTN_WIDTH).astype(BF16)
    silu_pg = _silu(proj(OFF_PG, D_MODEL))
    gate_a = gate(0)

    t = i * tile + lax.broadcasted_iota(jnp.int32, (tile, 1), 0)
    slabs_per_group = POOL_GDIM // LANES
    ys = []
    for g, w in enumerate(POOL_WINDOWS):
        half = w // 2
        inv_cnt = 1.0 / (jnp.minimum(t + half, seq) - jnp.maximum(t - half, 0)).astype(F32)
        ps = []
        for s in range(g * slabs_per_group, (g + 1) * slabs_per_group):
            win = uext_ref[s, pl.ds(POOL_HALO - half, tile, stride=1), :]
            for jj in range(1, w):
                win = win + uext_ref[s, pl.ds(POOL_HALO - half + jj, tile, stride=1), :]
            ps.append(win * inv_cnt - pv[:, s * LANES:(s + 1) * LANES])
        p = jnp.concatenate(ps, axis=1).astype(BF16)
        ys.append(_dot(p, _unpack_rows(wpg_ref[g])))
    ya_in = (jnp.concatenate(ys, axis=1) * psc_ref[...] * silu_pg).astype(BF16)

    es, ls = [], []
    for h in range(ATTN_HEADS):
        kh = kv_ref[:, h * HEAD_DIM:(h + 1) * HEAD_DIM]
        s = lax.dot_general(q[:, h * HEAD_DIM:(h + 1) * HEAD_DIM], kh, (((1,), (1,)), ((), ())),
                            preferred_element_type=F32) * (1.0 / math.sqrt(HEAD_DIM))
        e = jnp.exp(s - jnp.max(s, axis=-1, keepdims=True))
        ls.append(jnp.sum(e, axis=-1, keepdims=True))
        es.append(e.astype(BF16))

    yb_in = (f_ref[...] * _silu(proj(OFF_FG, FOUR_WIDTH))).astype(BF16)
    gate_b = gate(1)

    ya = _dot(ya_in, _unpack_rows(wpo_ref[...]))
    merged = gate_a * ya

    silu_ag = _silu(proj(OFF_AG, ATTN_WIDTH))
    o = jnp.concatenate(
        [_dot(es[h], kv_ref[:, ATTN_WIDTH + h * HEAD_DIM:ATTN_WIDTH + (h + 1) * HEAD_DIM]) / ls[h]
         for h in range(ATTN_HEADS)], axis=1)
    yc_in = (o * silu_ag).astype(BF16)

    yb = _dot(yb_in, _unpack_rows(wfo_ref[...]))
    merged = merged + gate_b * yb
    gate_c = gate(2)
    yc = _dot(yc_in, _unpack_rows(wao_ref[...]))
    merged = merged + gate_c * yc

    hres = x + _dot(merged.astype(BF16), _unpack_rows(wo_ref[...]))
    out_ref[...] = _rmsnorm(hres, gf_ref[...])


def _main_call(x, f, kv, norm_in, w_in_pk, w_pool_grp_pk, pool_scale,
               w_pool_out_pk, w_four_out_pk, w_attn_out_pk, b_gate, w_o_pk, norm_f):
    b, seq, _ = x.shape
    tile = TOKEN_TILE
    nt = seq // tile
    hb = tile // POOL_HALO
    nhb = seq // POOL_HALO
    kern = functools.partial(_main_kernel, seq=seq, tile=tile)
    return pl.pallas_call(
        kern,
        grid=(b, nt),
        in_specs=[pl.BlockSpec((None, tile, D_MODEL), lambda bi, i: (bi, i, 0)),
                  pl.BlockSpec((None, POOL_HALO, D_MODEL), lambda bi, i: (bi, jnp.maximum(i * hb - 1, 0), 0)),
                  pl.BlockSpec((None, POOL_HALO, D_MODEL),
                               lambda bi, i: (bi, jnp.minimum((i + 1) * hb, nhb - 1), 0)),
                  pl.BlockSpec((None, tile, FOUR_WIDTH), lambda bi, i: (bi, i, 0)),
                  pl.BlockSpec((None, N_MEM, 2 * ATTN_WIDTH), lambda bi, i: (bi, 0, 0)),
                  _const_spec((1, D_MODEL)),
                  _const_spec(w_in_pk.shape),
                  _const_spec(w_pool_grp_pk.shape),
                  _const_spec((1, D_MODEL)),
                  _const_spec(w_pool_out_pk.shape),
                  _const_spec(w_four_out_pk.shape),
                  _const_spec(w_attn_out_pk.shape),
                  _const_spec((1, 3 * D_MODEL)),
                  _const_spec(w_o_pk.shape),
                  _const_spec((1, D_MODEL))],
        out_specs=pl.BlockSpec((None, tile, D_MODEL), lambda bi, i: (bi, i, 0)),
        out_shape=jax.ShapeDtypeStruct((b, seq, D_MODEL), F32),
        scratch_shapes=[pltpu.VMEM((D_MODEL // LANES, tile + 2 * POOL_HALO, LANES), F32)],
        compiler_params=pltpu.CompilerParams(dimension_semantics=("arbitrary", "arbitrary"),
                                             vmem_limit_bytes=VMEM_LIMIT),
        name="encoder_main",
    )(x, x, x, f, kv, norm_in, w_in_pk, w_pool_grp_pk, pool_scale,
      w_pool_out_pk, w_four_out_pk, w_attn_out_pk, b_gate, w_o_pk, norm_f)


def _trunk(x, mem, norm_in, norm_mem, w_in_pk, w_pool_grp_pk, pool_scale, chan_pk, w_kv_pk,
           w_pool_out_pk, w_four_out_pk, w_attn_out_pk, b_gate, w_o_pk, norm_f):
    b, seq, _ = x.shape
    stage_a, tw, stage_b = _dft_constants(seq)
    stage_a, stage_b = (jnp.asarray(a).astype(BF16) for a in (stage_a, stage_b))
    kv = _kv_call(mem, norm_mem, w_kv_pk)
    t = _fourier_a_call(x, norm_in, w_in_pk, chan_pk, stage_a, tw)
    f = _fourier_b_call(t, stage_b, seq)
    f = jnp.swapaxes(f, 1, 2).reshape(b, seq, FOUR_WIDTH)
    return _main_call(x, f, kv, norm_in, w_in_pk, w_pool_grp_pk, pool_scale,
                      w_pool_out_pk, w_four_out_pk, w_attn_out_pk, b_gate, w_o_pk, norm_f)


def kernel(x_prompt, x_sample, mem_prompt, mem_sample, norm_in, norm_mem, w_in, w_pool_grp, pool_scale,
           w_four_grp, w_kv, w_pool_out, w_four_out, w_attn_out, b_gate, w_o, norm_f):
    assert norm_in.shape[0] == 1, "single-layer trunk"
    w_in_pk, w_pool_grp_pk, w_kv_pk, w_pool_out_pk, w_four_out_pk, w_attn_out_pk, w_o_pk = _pack_rows(
        w_in[0], w_pool_grp[0], w_kv[0], w_pool_out[0], w_four_out[0], w_attn_out[0], w_o[0])
    shared = (norm_in, norm_mem, w_in_pk, w_pool_grp_pk, pool_scale, _fold_chan_call(w_four_grp[0]), w_kv_pk,
              w_pool_out_pk, w_four_out_pk, w_attn_out_pk, b_gate, w_o_pk, norm_f.reshape(1, D_MODEL))
    return (_trunk(x_prompt, mem_prompt, *shared), _trunk(x_sample, mem_sample, *shared))
```

```python
import functools
import math

import jax
import jax.numpy as jnp
import numpy as np
from jax import lax
from jax.experimental import pallas as pl
from jax.experimental.pallas import tpu as pltpu

D_MODEL = 1024
EPS = 1e-6
POOL_WINDOWS = (2, 4, 8, 16)
POOL_GDIM = 256
POOL_HALO = 8
FOUR_GROUPS = 4
FOUR_GDIM = 128
FOUR_WIDTH = FOUR_GROUPS * FOUR_GDIM
ATTN_HEADS = 4
HEAD_DIM = 128
ATTN_WIDTH = ATTN_HEADS * HEAD_DIM
N_MEM = 256
OFF_PV, OFF_PG, OFF_FV, OFF_FG, OFF_Q, OFF_AG, OFF_MG = 0, 1024, 2048, 2560, 3072, 3584, 4096

SUBLANES = 8
LANES = 128
DFT_N1 = 32
TOKEN_TILE = 512
FOURIER_SMALL_T_BYTES = 8 * 1024 * 1024
PACK_STEPS = 8
VMEM_LIMIT = 56 * 1024 * 1024

BF16 = jnp.bfloat16
F32 = jnp.float32


def _dot(a, b):
    return jnp.dot(a, b, preferred_element_type=F32)


def _pack_kernel(*refs):
    n = len(refs) // 2
    for w_ref, o_ref in zip(refs[:n], refs[n:]):
        o_ref[...] = pltpu.bitcast(w_ref[...].astype(BF16), jnp.uint32)


def _pack_rows(*ws):
    flat = [w.reshape(-1, w.shape[-1]) for w in ws]
    in_specs, out_specs, out_shape = [], [], []
    for w in flat:
        k, n = w.shape
        in_specs.append(pl.BlockSpec((k // PACK_STEPS, n), lambda i: (i, 0)))
        out_specs.append(pl.BlockSpec((k // (2 * PACK_STEPS), n), lambda i: (i, 0)))
        out_shape.append(jax.ShapeDtypeStruct((k // 2, n), jnp.uint32))
    packed = pl.pallas_call(
        _pack_kernel,
        grid=(PACK_STEPS,),
        in_specs=in_specs,
        out_specs=out_specs,
        out_shape=out_shape,
        compiler_params=pltpu.CompilerParams(dimension_semantics=("arbitrary",)),
        name="pack_weights",
    )(*flat)
    return [p.reshape(*w.shape[:-2], w.shape[-2] // 2, w.shape[-1]) for p, w in zip(packed, ws)]


def _unpack_rows(w_u32):
    return pltpu.bitcast(w_u32, BF16)


def _rmsnorm(x, g):
    r = lax.rsqrt(jnp.mean(x * x, axis=-1, keepdims=True) + EPS)
    return x * r * g


def _sigmoid(x):
    return 0.5 * jnp.tanh(0.5 * x) + 0.5


def _silu(x):
    return x * _sigmoid(x)


def _const_spec(shape):
    nd = len(shape)
    return pl.BlockSpec(shape, lambda *_: (0,) * nd, pipeline_mode=pl.Buffered(1))


@functools.cache
def _channel_dft():
    c = np.arange(FOUR_GDIM)
    ang = 2.0 * np.pi * np.outer(c, c) / FOUR_GDIM
    return np.concatenate([np.cos(ang), -np.sin(ang)], axis=0).astype(np.float32)


@functools.cache
def _dft_constants(seq):
    n1 = DFT_N1
    n2 = seq // n1
    k1 = np.arange(n1)
    psi = 2.0 * np.pi * np.outer(k1, k1) / n1
    eye = np.eye(SUBLANES)
    mr = np.kron(np.cos(psi), eye)
    mi = np.kron(-np.sin(psi), eye)
    stage_a = np.stack([np.concatenate([mr, -mi], axis=1),
                        np.concatenate([mi, mr], axis=1)], axis=0)
    j = np.arange(n2 // SUBLANES)
    r = np.arange(SUBLANES)
    s2 = (SUBLANES * j[:, None, None] + r[None, None, :])
    phi = 2.0 * np.pi * k1[None, :, None] * s2 / seq
    phi = phi.reshape(len(j), n1 * SUBLANES, 1)
    tw = np.concatenate([np.broadcast_to(np.cos(phi), phi.shape[:2] + (128,)),
                         np.broadcast_to(-np.sin(phi), phi.shape[:2] + (128,))], axis=2)
    k2 = np.arange(n2)
    th = 2.0 * np.pi * np.outer(k2, k2) / n2
    stage_b = np.concatenate([np.cos(th), np.sin(th)], axis=1)
    return tuple(np.asarray(a, np.float32) for a in (stage_a, tw, stage_b))


def _fold_chan_kernel(c_ref, wg_ref, o_ref):
    for h in range(FOUR_GROUPS):
        r = jnp.dot(c_ref[...], wg_ref[h], precision=lax.Precision.HIGHEST, preferred_element_type=F32)
        folded = jnp.concatenate([r[:FOUR_GDIM], r[FOUR_GDIM:]], axis=1)
        o_ref[h] = pltpu.bitcast(folded.astype(BF16), jnp.uint32)


def _fold_chan_call(w_four_grp):
    return pl.pallas_call(
        _fold_chan_kernel,
        out_shape=jax.ShapeDtypeStruct((FOUR_GROUPS, FOUR_GDIM // 2, 2 * FOUR_GDIM), jnp.uint32),
        name="fold_chan",
    )(jnp.asarray(_channel_dft()), w_four_grp)


def _kv_kernel(mem_ref, g_ref, wkv_ref, kv_ref):
    mn = _rmsnorm(mem_ref[...], g_ref[...]).astype(BF16)
    kv_ref[...] = _dot(mn, _unpack_rows(wkv_ref[...])).astype(BF16)


def _kv_call(mem, norm_mem, w_kv_pk):
    b = mem.shape[0]
    return pl.pallas_call(
        _kv_kernel,
        grid=(b,),
        in_specs=[pl.BlockSpec((None, N_MEM, D_MODEL), lambda i: (i, 0, 0)),
                  _const_spec((1, D_MODEL)),
                  _const_spec(w_kv_pk.shape)],
        out_specs=pl.BlockSpec((None, N_MEM, 2 * ATTN_WIDTH), lambda i: (i, 0, 0)),
        out_shape=jax.ShapeDtypeStruct((b, N_MEM, 2 * ATTN_WIDTH), BF16),
        compiler_params=pltpu.CompilerParams(dimension_semantics=("arbitrary",)),
        name="kv_proj",
    )(mem, norm_mem, w_kv_pk)


def _fourier_kernel(x_ref, g_ref, wfv_ref, chan_ref, sta_ref, tw_ref, stb_ref, f_ref, t_scr, *,
                    row_groups, a_steps, k1_per_step, scale):
    j = pl.program_id(1)
    step_rows = row_groups * SUBLANES
    kron_rows = DFT_N1 * SUBLANES

    @pl.when(j < a_steps)
    def _stage_a():
        x = x_ref[...].reshape(DFT_N1 * step_rows, D_MODEL)
        xn = _rmsnorm(x, g_ref[...]).astype(BF16)
        fv = _dot(xn, _unpack_rows(wfv_ref[...])).astype(BF16)
        zr, zi = [], []
        for h in range(FOUR_GROUPS):
            z = _dot(fv[:, h * FOUR_GDIM:(h + 1) * FOUR_GDIM], _unpack_rows(chan_ref[h]))
            zr.append(z[:, :FOUR_GDIM])
            zi.append(z[:, FOUR_GDIM:])
        zr = jnp.concatenate(zr, axis=1).reshape(DFT_N1, step_rows, FOUR_WIDTH)
        zi = jnp.concatenate(zi, axis=1).reshape(DFT_N1, step_rows, FOUR_WIDTH)
        outs = []
        for g in range(row_groups):
            rs = slice(g * SUBLANES, (g + 1) * SUBLANES)
            zcat = jnp.concatenate([zr[:, rs, :].reshape(kron_rows, FOUR_WIDTH),
                                    zi[:, rs, :].reshape(kron_rows, FOUR_WIDTH)], axis=0).astype(BF16)
            tr = _dot(sta_ref[0], zcat)
            ti = _dot(sta_ref[1], zcat)
            twr = tw_ref[g, :, :FOUR_GDIM]
            twi = tw_ref[g, :, FOUR_GDIM:]
            re, im = [], []
            for h in range(FOUR_GROUPS):
                a, b = tr[:, h * FOUR_GDIM:(h + 1) * FOUR_GDIM], ti[:, h * FOUR_GDIM:(h + 1) * FOUR_GDIM]
                re.append(a * twr - b * twi)
                im.append(a * twi + b * twr)
            outs.append(jnp.concatenate(re + im, axis=1).reshape(DFT_N1, SUBLANES, 2 * FOUR_WIDTH))
        row0 = pl.multiple_of(j * step_rows, step_rows)
        t_scr[:, pl.ds(row0, step_rows), :] = jnp.concatenate(outs, axis=1).astype(BF16)

    @pl.when(j >= a_steps)
    def _stage_b():
        k0 = (j - a_steps) * k1_per_step
        for k in range(k1_per_step):
            t = t_scr[k0 + k]
            tcat = jnp.concatenate([t[:, :FOUR_WIDTH], t[:, FOUR_WIDTH:]], axis=0)
            f_ref[k] = (_dot(stb_ref[...], tcat) * scale).astype(BF16)


def _fourier_call(x, norm_in, w_in_pk, chan, stage_a, tw, stage_b):
    b, seq, _ = x.shape
    n1, n2 = DFT_N1, seq // DFT_N1
    t_bytes = seq * 2 * FOUR_WIDTH * 2
    row_groups = 4 if t_bytes <= FOURIER_SMALL_T_BYTES else 2
    step_rows = row_groups * SUBLANES
    kron_rows = n1 * SUBLANES
    a_steps = n2 // step_rows
    k1_per_step = max(1, 1024 // n2)
    b_steps = n1 // k1_per_step
    kern = functools.partial(_fourier_kernel, row_groups=row_groups, a_steps=a_steps, k1_per_step=k1_per_step,
                             scale=1.0 / math.sqrt(seq * FOUR_GDIM))
    x4 = x.reshape(b, n1, n2, D_MODEL)
    return pl.pallas_call(
        kern,
        grid=(b, a_steps + b_steps),
        in_specs=[pl.BlockSpec((None, n1, step_rows, D_MODEL), lambda i, j: (i, 0, jnp.minimum(j, a_steps - 1), 0)),
                  _const_spec((1, D_MODEL)),
                  pl.BlockSpec((D_MODEL // 2, FOUR_WIDTH), lambda i, j: (0, OFF_FV // FOUR_WIDTH),
                               pipeline_mode=pl.Buffered(1)),
                  _const_spec(chan.shape),
                  _const_spec((2, kron_rows, 2 * kron_rows)),
                  pl.BlockSpec((row_groups, kron_rows, 2 * FOUR_GDIM), lambda i, j: (jnp.minimum(j, a_steps - 1), 0, 0)),
                  _const_spec((n2, 2 * n2))],
        out_specs=pl.BlockSpec((None, k1_per_step, n2, FOUR_WIDTH),
                               lambda i, j: (i, jnp.maximum(j - a_steps, 0), 0, 0)),
        out_shape=jax.ShapeDtypeStruct((b, n1, n2, FOUR_WIDTH), BF16),
        scratch_shapes=[pltpu.VMEM((n1, n2, 2 * FOUR_WIDTH), BF16)],
        compiler_params=pltpu.CompilerParams(dimension_semantics=("arbitrary", "arbitrary"),
                                             vmem_limit_bytes=VMEM_LIMIT),
        name="fourier",
    )(x4, norm_in, w_in_pk, chan, stage_a, tw, stage_b)


def _main_kernel(xm_ref, xp_ref, xq_ref, f_ref, kv_ref, gin_ref, win_ref, wpg_ref, psc_ref,
                 wpo_ref, wfo_ref, wao_ref, bg_ref, wo_ref, gf_ref, out_ref, uext_ref, *, seq, tile):
    i = pl.program_id(1)
    last = pl.num_programs(1) - 1
    gin = gin_ref[...]
    x = xm_ref[...]
    xn = _rmsnorm(x, gin).astype(BF16)

    def proj(off, width, lhs=xn):
        return _dot(lhs, _unpack_rows(win_ref[:, off:off + width]))

    def gate(branch):
        off = branch * D_MODEL
        return _sigmoid(proj(OFF_MG + off, D_MODEL) + bg_ref[:, off:off + D_MODEL])


    halo = jnp.concatenate([xp_ref[...], xq_ref[...]], axis=0)
    pv_ext = proj(OFF_PV, D_MODEL, jnp.concatenate([xn, _rmsnorm(halo, gin).astype(BF16)], axis=0))
    pv = pv_ext[:tile]
    pv_prev = jnp.where(i > 0, pv_ext[tile:tile + POOL_HALO], 0.0)
    pv_next = jnp.where(i < last, pv_ext[tile + POOL_HALO:], 0.0)
    for s in range(D_MODEL // LANES):
        lanes = slice(s * LANES, (s + 1) * LANES)
        uext_ref[s, 0:POOL_HALO, :] = pv_prev[:, lanes]
        uext_ref[s, POOL_HALO:POOL_HALO + tile, :] = pv[:, lanes]
        uext_ref[s, POOL_HALO + tile:, :] = pv_next[:, lanes]

    q = proj(OFF_Q, ATTN_WIDTH).astype(BF16)
    silu_pg = _silu(proj(OFF_PG, D_MODEL))
    gate_a = gate(0)

    t = i * tile + lax.broadcasted_iota(jnp.int32, (tile, 1), 0)
    slabs_per_group = POOL_GDIM // LANES
    ys = []
    for g, w in enumerate(POOL_WINDOWS):
        half = w // 2
        inv_cnt = 1.0 / (jnp.minimum(t + half, seq) - jnp.maximum(t - half, 0)).astype(F32)
        ps = []
        for s in range(g * slabs_per_group, (g + 1) * slabs_per_group):
            win = uext_ref[s, pl.ds(POOL_HALO - half, tile, stride=1), :]
            for jj in range(1, w):
                win = win + uext_ref[s, pl.ds(POOL_HALO - half + jj, tile, stride=1), :]
            ps.append(win * inv_cnt - pv[:, s * LANES:(s + 1) * LANES])
        p = jnp.concatenate(ps, axis=1).astype(BF16)
        ys.append(_dot(p, _unpack_rows(wpg_ref[g])))
    ya_in = (jnp.concatenate(ys, axis=1) * psc_ref[...] * silu_pg).astype(BF16)

    es, ls = [], []
    for h in range(ATTN_HEADS):
        kh = kv_ref[:, h * HEAD_DIM:(h + 1) * HEAD_DIM]
        s = lax.dot_general(q[:, h * HEAD_DIM:(h + 1) * HEAD_DIM], kh, (((1,), (1,)), ((), ())),
                            preferred_element_type=F32) * (1.0 / math.sqrt(HEAD_DIM))
        e = jnp.exp(s - jnp.max(s, axis=-1, keepdims=True))
        ls.append(jnp.sum(e, axis=-1, keepdims=True))
        es.append(e.astype(BF16))

    yb_in = (f_ref[...] * _silu(proj(OFF_FG, FOUR_WIDTH))).astype(BF16)
    gate_b = gate(1)

    ya = _dot(ya_in, _unpack_rows(wpo_ref[...]))
    merged = gate_a * ya

    silu_ag = _silu(proj(OFF_AG, ATTN_WIDTH))
    o = jnp.concatenate(
        [_dot(es[h], kv_ref[:, ATTN_WIDTH + h * HEAD_DIM:ATTN_WIDTH + (h + 1) * HEAD_DIM]) / ls[h]
         for h in range(ATTN_HEADS)], axis=1)
    yc_in = (o * silu_ag).astype(BF16)

    yb = _dot(yb_in, _unpack_rows(wfo_ref[...]))
    merged = merged + gate_b * yb
    gate_c = gate(2)
    yc = _dot(yc_in, _unpack_rows(wao_ref[...]))
    merged = merged + gate_c * yc

    hres = x + _dot(merged.astype(BF16), _unpack_rows(wo_ref[...]))
    out_ref[...] = _rmsnorm(hres, gf_ref[...])


def _main_call(x, f, kv, norm_in, w_in_pk, w_pool_grp_pk, pool_scale,
               w_pool_out_pk, w_four_out_pk, w_attn_out_pk, b_gate, w_o_pk, norm_f):
    b, seq, _ = x.shape
    tile = TOKEN_TILE
    nt = seq // tile
    hb = tile // POOL_HALO
    nhb = seq // POOL_HALO
    kern = functools.partial(_main_kernel, seq=seq, tile=tile)
    return pl.pallas_call(
        kern,
        grid=(b, nt),
        in_specs=[pl.BlockSpec((None, tile, D_MODEL), lambda bi, i: (bi, i, 0)),
                  pl.BlockSpec((None, POOL_HALO, D_MODEL), lambda bi, i: (bi, jnp.maximum(i * hb - 1, 0), 0)),
                  pl.BlockSpec((None, POOL_HALO, D_MODEL),
                               lambda bi, i: (bi, jnp.minimum((i + 1) * hb, nhb - 1), 0)),
                  pl.BlockSpec((None, tile, FOUR_WIDTH), lambda bi, i: (bi, i, 0)),
                  pl.BlockSpec((None, N_MEM, 2 * ATTN_WIDTH), lambda bi, i: (bi, 0, 0)),
                  _const_spec((1, D_MODEL)),
                  _const_spec(w_in_pk.shape),
                  _const_spec(w_pool_grp_pk.shape),
                  _const_spec((1, D_MODEL)),
                  _const_spec(w_pool_out_pk.shape),
                  _const_spec(w_four_out_pk.shape),
                  _const_spec(w_attn_out_pk.shape),
                  _const_spec((1, 3 * D_MODEL)),
                  _const_spec(w_o_pk.shape),
                  _const_spec((1, D_MODEL))],
        out_specs=pl.BlockSpec((None, tile, D_MODEL), lambda bi, i: (bi, i, 0)),
        out_shape=jax.ShapeDtypeStruct((b, seq, D_MODEL), F32),
        scratch_shapes=[pltpu.VMEM((D_MODEL // LANES, tile + 2 * POOL_HALO, LANES), F32)],
        compiler_params=pltpu.CompilerParams(dimension_semantics=("arbitrary", "arbitrary"),
                                             vmem_limit_bytes=VMEM_LIMIT),
        name="encoder_main",
    )(x, x, x, f, kv, norm_in, w_in_pk, w_pool_grp_pk, pool_scale,
      w_pool_out_pk, w_four_out_pk, w_attn_out_pk, b_gate, w_o_pk, norm_f)


def _trunk(x, mem, norm_in, norm_mem, w_in_pk, w_pool_grp_pk, pool_scale, chan_pk, w_kv_pk,
           w_pool_out_pk, w_four_out_pk, w_attn_out_pk, b_gate, w_o_pk, norm_f):
    b, seq, _ = x.shape
    stage_a, tw, stage_b = _dft_constants(seq)
    stage_a, stage_b = (jnp.asarray(a).astype(BF16) for a in (stage_a, stage_b))
    kv = _kv_call(mem, norm_mem, w_kv_pk)
    f = _fourier_call(x, norm_in, w_in_pk, chan_pk, stage_a, tw, stage_b)
    f = jnp.swapaxes(f, 1, 2).reshape(b, seq, FOUR_WIDTH)
    return _main_call(x, f, kv, norm_in, w_in_pk, w_pool_grp_pk, pool_scale,
                      w_pool_out_pk, w_four_out_pk, w_attn_out_pk, b_gate, w_o_pk, norm_f)


def kernel(x_prompt, x_sample, mem_prompt, mem_sample, norm_in, norm_mem, w_in, w_pool_grp, pool_scale,
           w_four_grp, w_kv, w_pool_out, w_four_out, w_attn_out, b_gate, w_o, norm_f):
    assert norm_in.shape[0] == 1, "single-layer trunk"
    w_in_pk, w_pool_grp_pk, w_kv_pk, w_pool_out_pk, w_four_out_pk, w_attn_out_pk, w_o_pk = _pack_rows(
        w_in[0], w_pool_grp[0], w_kv[0], w_pool_out[0], w_four_out[0], w_attn_out[0], w_o[0])
    shared = (norm_in, norm_mem, w_in_pk, w_pool_grp_pk, pool_scale, _fold_chan_call(w_four_grp[0]), w_kv_pk,
              w_pool_out_pk, w_four_out_pk, w_attn_out_pk, b_gate, w_o_pk, norm_f.reshape(1, D_MODEL))
    return (_trunk(x_prompt, mem_prompt, *shared), _trunk(x_sample, mem_sample, *shared))
```

```python
import functools
import math

import jax
import jax.numpy as jnp
import numpy as np
from jax import lax
from jax.experimental import pallas as pl
from jax.experimental.pallas import tpu as pltpu

D_MODEL = 1024
EPS = 1e-6
POOL_WINDOWS = (2, 4, 8, 16)
POOL_GDIM = 256
POOL_HALO = 8
FOUR_GROUPS = 4
FOUR_GDIM = 128
FOUR_WIDTH = FOUR_GROUPS * FOUR_GDIM
ATTN_HEADS = 4
HEAD_DIM = 128
ATTN_WIDTH = ATTN_HEADS * HEAD_DIM
N_MEM = 256
OFF_PV, OFF_PG, OFF_FV, OFF_FG, OFF_Q, OFF_AG, OFF_MG = 0, 1024, 2048, 2560, 3072, 3584, 4096

SUBLANES = 8
LANES = 128
KRON_DEPTH = 256
FOURIER_A_STEP_TOKENS = 1024
FOURIER_B_STEP_TOKENS = 1024
TOKEN_TILE = 512
PACK_STEPS = 8
VMEM_LIMIT = 56 * 1024 * 1024

BF16 = jnp.bfloat16
F32 = jnp.float32


def _dot(a, b):
    return jnp.dot(a, b, preferred_element_type=F32)


def _pack_kernel(*refs):
    n = len(refs) // 2
    for w_ref, o_ref in zip(refs[:n], refs[n:]):
        o_ref[...] = pltpu.bitcast(w_ref[...].astype(BF16), jnp.uint32)


def _pack_rows(*ws):
    flat = [w.reshape(-1, w.shape[-1]) for w in ws]
    in_specs, out_specs, out_shape = [], [], []
    for w in flat:
        k, n = w.shape
        in_specs.append(pl.BlockSpec((k // PACK_STEPS, n), lambda i: (i, 0)))
        out_specs.append(pl.BlockSpec((k // (2 * PACK_STEPS), n), lambda i: (i, 0)))
        out_shape.append(jax.ShapeDtypeStruct((k // 2, n), jnp.uint32))
    packed = pl.pallas_call(
        _pack_kernel,
        grid=(PACK_STEPS,),
        in_specs=in_specs,
        out_specs=out_specs,
        out_shape=out_shape,
        compiler_params=pltpu.CompilerParams(dimension_semantics=("arbitrary",)),
        name="pack_weights",
    )(*flat)
    return [p.reshape(*w.shape[:-2], w.shape[-2] // 2, w.shape[-1]) for p, w in zip(packed, ws)]


def _unpack_rows(w_u32):
    return pltpu.bitcast(w_u32, BF16)


def _rmsnorm(x, g):
    r = lax.rsqrt(jnp.mean(x * x, axis=-1, keepdims=True) + EPS)
    return x * r * g


def _sigmoid(x):
    return 0.5 * jnp.tanh(0.5 * x) + 0.5


def _silu(x):
    return x * _sigmoid(x)


def _const_spec(shape):
    nd = len(shape)
    return pl.BlockSpec(shape, lambda *_: (0,) * nd, pipeline_mode=pl.Buffered(1))


@functools.cache
def _channel_dft():
    c = np.arange(FOUR_GDIM)
    ang = 2.0 * np.pi * np.outer(c, c) / FOUR_GDIM
    return np.concatenate([np.cos(ang), -np.sin(ang)], axis=0).astype(np.float32)


def _dft_n1(seq):
    return min(KRON_DEPTH // SUBLANES, seq // 128)


@functools.cache
def _dft_constants(seq):
    n1 = _dft_n1(seq)
    n2 = seq // n1
    group = KRON_DEPTH // n1
    k1 = np.arange(n1)
    psi = 2.0 * np.pi * np.outer(k1, k1) / n1
    eye = np.eye(group)
    mr = np.kron(np.cos(psi), eye)
    mi = np.kron(-np.sin(psi), eye)
    stage_a = np.stack([np.concatenate([mr, -mi], axis=1),
                        np.concatenate([mi, mr], axis=1)], axis=0)
    j = np.arange(n2 // group)
    r = np.arange(group)
    s2 = (group * j[:, None, None] + r[None, None, :])
    phi = 2.0 * np.pi * k1[None, :, None] * s2 / seq
    phi = phi.reshape(len(j), KRON_DEPTH, 1)
    tw = np.concatenate([np.broadcast_to(np.cos(phi), phi.shape[:2] + (128,)),
                         np.broadcast_to(-np.sin(phi), phi.shape[:2] + (128,))], axis=2)
    k2 = np.arange(n2)
    th = 2.0 * np.pi * np.outer(k2, k2) / n2
    stage_b = np.concatenate([np.cos(th), np.sin(th)], axis=1)
    return tuple(np.asarray(a, np.float32) for a in (stage_a, tw, stage_b))


def _fold_chan_kernel(c_ref, wg_ref, o_ref):
    for h in range(FOUR_GROUPS):
        r = jnp.dot(c_ref[...], wg_ref[h], precision=lax.Precision.HIGHEST, preferred_element_type=F32)
        folded = jnp.concatenate([r[:FOUR_GDIM], r[FOUR_GDIM:]], axis=1)
        o_ref[h] = pltpu.bitcast(folded.astype(BF16), jnp.uint32)


def _fold_chan_call(w_four_grp):
    return pl.pallas_call(
        _fold_chan_kernel,
        out_shape=jax.ShapeDtypeStruct((FOUR_GROUPS, FOUR_GDIM // 2, 2 * FOUR_GDIM), jnp.uint32),
        name="fold_chan",
    )(jnp.asarray(_channel_dft()), w_four_grp)


def _fourier_a_kernel(x_ref, g_ref, wfv_ref, chan_ref, sta_ref, tw_ref, t_ref, *, n1, row_groups):
    group = KRON_DEPTH // n1
    step_rows = row_groups * group
    x = x_ref[...].reshape(n1 * step_rows, D_MODEL)
    xn = _rmsnorm(x, g_ref[...]).astype(BF16)
    fv = _dot(xn, _unpack_rows(wfv_ref[...])).astype(BF16)
    zr, zi = [], []
    for h in range(FOUR_GROUPS):
        z = _dot(fv[:, h * FOUR_GDIM:(h + 1) * FOUR_GDIM], _unpack_rows(chan_ref[h]))
        zr.append(z[:, :FOUR_GDIM])
        zi.append(z[:, FOUR_GDIM:])
    zr = jnp.concatenate(zr, axis=1).reshape(n1, step_rows, FOUR_WIDTH)
    zi = jnp.concatenate(zi, axis=1).reshape(n1, step_rows, FOUR_WIDTH)
    outs = []
    for g in range(row_groups):
        rs = slice(g * group, (g + 1) * group)
        zcat = jnp.concatenate([zr[:, rs, :].reshape(KRON_DEPTH, FOUR_WIDTH),
                                zi[:, rs, :].reshape(KRON_DEPTH, FOUR_WIDTH)], axis=0).astype(BF16)
        tr = _dot(sta_ref[0], zcat)
        ti = _dot(sta_ref[1], zcat)
        twr = tw_ref[g, :, :FOUR_GDIM]
        twi = tw_ref[g, :, FOUR_GDIM:]
        re, im = [], []
        for h in range(FOUR_GROUPS):
            a, b = tr[:, h * FOUR_GDIM:(h + 1) * FOUR_GDIM], ti[:, h * FOUR_GDIM:(h + 1) * FOUR_GDIM]
            re.append(a * twr - b * twi)
            im.append(a * twi + b * twr)
        outs.append(jnp.concatenate(re + im, axis=1).reshape(n1, group, 2 * FOUR_WIDTH))
    t_ref[...] = jnp.concatenate(outs, axis=1).astype(BF16)


def _fourier_a_call(x, norm_in, w_in_pk, chan, stage_a, tw):
    b, seq, _ = x.shape
    n1 = _dft_n1(seq)
    n2 = seq // n1
    step_rows = FOURIER_A_STEP_TOKENS // n1
    row_groups = step_rows // (KRON_DEPTH // n1)
    x4 = x.reshape(b, n1, n2, D_MODEL)
    return pl.pallas_call(
        functools.partial(_fourier_a_kernel, n1=n1, row_groups=row_groups),
        grid=(b, n2 // step_rows),
        in_specs=[pl.BlockSpec((None, n1, step_rows, D_MODEL), lambda i, j: (i, 0, j, 0)),
                  _const_spec((1, D_MODEL)),
                  pl.BlockSpec((D_MODEL // 2, FOUR_WIDTH), lambda i, j: (0, OFF_FV // FOUR_WIDTH),
                               pipeline_mode=pl.Buffered(1)),
                  _const_spec(chan.shape),
                  _const_spec((2, KRON_DEPTH, 2 * KRON_DEPTH)),
                  pl.BlockSpec((row_groups, KRON_DEPTH, 2 * FOUR_GDIM), lambda i, j: (j, 0, 0))],
        out_specs=pl.BlockSpec((None, n1, step_rows, 2 * FOUR_WIDTH), lambda i, j: (i, 0, j, 0)),
        out_shape=jax.ShapeDtypeStruct((b, n1, n2, 2 * FOUR_WIDTH), BF16),
        compiler_params=pltpu.CompilerParams(dimension_semantics=("arbitrary", "arbitrary")),
        name="fourier_a",
    )(x4, norm_in, w_in_pk, chan, stage_a, tw)


def _fourier_b_kernel(t_ref, stb_ref, f_ref, *, k1_per_step, scale):
    for k in range(k1_per_step):
        t = t_ref[k]
        tcat = jnp.concatenate([t[:, :FOUR_WIDTH], t[:, FOUR_WIDTH:]], axis=0)
        f_ref[k] = (_dot(stb_ref[...], tcat) * scale).astype(BF16)


def _fourier_b_call(t, stage_b, seq):
    b, n1, n2, _ = t.shape
    k1_per_step = max(1, FOURIER_B_STEP_TOKENS // n2)
    kern = functools.partial(_fourier_b_kernel, k1_per_step=k1_per_step,
                             scale=1.0 / math.sqrt(seq * FOUR_GDIM))
    return pl.pallas_call(
        kern,
        grid=(b, n1 // k1_per_step),
        in_specs=[pl.BlockSpec((None, k1_per_step, n2, 2 * FOUR_WIDTH), lambda i, j: (i, j, 0, 0)),
                  _const_spec((n2, 2 * n2))],
        out_specs=pl.BlockSpec((None, k1_per_step, n2, FOUR_WIDTH), lambda i, j: (i, j, 0, 0)),
        out_shape=jax.ShapeDtypeStruct((b, n1, n2, FOUR_WIDTH), BF16),
        compiler_params=pltpu.CompilerParams(dimension_semantics=("arbitrary", "arbitrary")),
        name="fourier_b",
    )(t, stage_b)


def _main_kernel(xm_ref, xp_ref, xq_ref, f_ref, mem_ref, gin_ref, gmem_ref, win_ref, wkv_ref, wpg_ref, psc_ref,
                 wpo_ref, wfo_ref, wao_ref, bg_ref, wo_ref, gf_ref, out_ref, uext_ref, kv_ref, *, seq, tile):
    i = pl.program_id(1)
    last = pl.num_programs(1) - 1

    @pl.when(i == 0)
    def _project_memory():
        mn = _rmsnorm(mem_ref[...], gmem_ref[...]).astype(BF16)
        kv_ref[...] = _dot(mn, _unpack_rows(wkv_ref[...])).astype(BF16)

    gin = gin_ref[...]
    x = xm_ref[...]
    xn = _rmsnorm(x, gin).astype(BF16)

    def proj(off, width, lhs=xn):
        return _dot(lhs, _unpack_rows(win_ref[:, off:off + width]))

    def gate(branch):
        off = branch * D_MODEL
        return _sigmoid(proj(OFF_MG + off, D_MODEL) + bg_ref[:, off:off + D_MODEL])


    halo = jnp.concatenate([xp_ref[...], xq_ref[...]], axis=0)
    pv_ext = proj(OFF_PV, D_MODEL, jnp.concatenate([xn, _rmsnorm(halo, gin).astype(BF16)], axis=0))
    pv = pv_ext[:tile]
    pv_prev = jnp.where(i > 0, pv_ext[tile:tile + POOL_HALO], 0.0)
    pv_next = jnp.where(i < last, pv_ext[tile + POOL_HALO:], 0.0)
    for s in range(D_MODEL // LANES):
        lanes = slice(s * LANES, (s + 1) * LANES)
        uext_ref[s, 0:POOL_HALO, :] = pv_prev[:, lanes]
        uext_ref[s, POOL_HALO:POOL_HALO + tile, :] = pv[:, lanes]
        uext_ref[s, POOL_HALO + tile:, :] = pv_next[:, lanes]

    q = proj(OFF_Q, ATTN_WIDTH).astype(BF16)
    silu_pg = _silu(proj(OFF_PG, D_MODEL))
    gate_a = gate(0)

    t = i * tile + lax.broadcasted_iota(jnp.int32, (tile, 1), 0)
    slabs_per_group = POOL_GDIM // LANES
    ys = []
    for g, w in enumerate(POOL_WINDOWS):
        half = w // 2
        inv_cnt = 1.0 / (jnp.minimum(t + half, seq) - jnp.maximum(t - half, 0)).astype(F32)
        ps = []
        for s in range(g * slabs_per_group, (g + 1) * slabs_per_group):
            win = uext_ref[s, pl.ds(POOL_HALO - half, tile, stride=1), :]
            for jj in range(1, w):
                win = win + uext_ref[s, pl.ds(POOL_HALO - half + jj, tile, stride=1), :]
            ps.append(win * inv_cnt - pv[:, s * LANES:(s + 1) * LANES])
        p = jnp.concatenate(ps, axis=1).astype(BF16)
        ys.append(_dot(p, _unpack_rows(wpg_ref[g])))
    ya_in = (jnp.concatenate(ys, axis=1) * psc_ref[...] * silu_pg).astype(BF16)

    es, ls = [], []
    for h in range(ATTN_HEADS):
        kh = kv_ref[:, h * HEAD_DIM:(h + 1) * HEAD_DIM]
        s = lax.dot_general(q[:, h * HEAD_DIM:(h + 1) * HEAD_DIM], kh, (((1,), (1,)), ((), ())),
                            preferred_element_type=F32) * (1.0 / math.sqrt(HEAD_DIM))
        e = jnp.exp(s - jnp.max(s, axis=-1, keepdims=True))
        ls.append(jnp.sum(e, axis=-1, keepdims=True))
        es.append(e.astype(BF16))

    yb_in = (f_ref[...] * _silu(proj(OFF_FG, FOUR_WIDTH))).astype(BF16)
    gate_b = gate(1)

    ya = _dot(ya_in, _unpack_rows(wpo_ref[...]))
    merged = gate_a * ya

    silu_ag = _silu(proj(OFF_AG, ATTN_WIDTH))
    o = jnp.concatenate(
        [_dot(es[h], kv_ref[:, ATTN_WIDTH + h * HEAD_DIM:ATTN_WIDTH + (h + 1) * HEAD_DIM]) / ls[h]
         for h in range(ATTN_HEADS)], axis=1)
    yc_in = (o * silu_ag).astype(BF16)

    yb = _dot(yb_in, _unpack_rows(wfo_ref[...]))
    merged = merged + gate_b * yb
    gate_c = gate(2)
    yc = _dot(yc_in, _unpack_rows(wao_ref[...]))
    merged = merged + gate_c * yc

    hres = x + _dot(merged.astype(BF16), _unpack_rows(wo_ref[...]))
    out_ref[...] = _rmsnorm(hres, gf_ref[...])


def _main_call(x, f, mem, norm_in, norm_mem, w_in_pk, w_kv_pk, w_pool_grp_pk, pool_scale,
               w_pool_out_pk, w_four_out_pk, w_attn_out_pk, b_gate, w_o_pk, norm_f):
    b, seq, _ = x.shape
    tile = TOKEN_TILE
    nt = seq // tile
    hb = tile // POOL_HALO
    nhb = seq // POOL_HALO
    kern = functools.partial(_main_kernel, seq=seq, tile=tile)
    return pl.pallas_call(
        kern,
        grid=(b, nt),
        in_specs=[pl.BlockSpec((None, tile, D_MODEL), lambda bi, i: (bi, i, 0)),
                  pl.BlockSpec((None, POOL_HALO, D_MODEL), lambda bi, i: (bi, jnp.maximum(i * hb - 1, 0), 0)),
                  pl.BlockSpec((None, POOL_HALO, D_MODEL),
                               lambda bi, i: (bi, jnp.minimum((i + 1) * hb, nhb - 1), 0)),
                  pl.BlockSpec((None, tile, FOUR_WIDTH), lambda bi, i: (bi, i, 0)),
                  pl.BlockSpec((None, N_MEM, D_MODEL), lambda bi, i: (bi, 0, 0)),
                  _const_spec((1, D_MODEL)),
                  _const_spec((1, D_MODEL)),
                  _const_spec(w_in_pk.shape),
                  _const_spec(w_kv_pk.shape),
                  _const_spec(w_pool_grp_pk.shape),
                  _const_spec((1, D_MODEL)),
                  _const_spec(w_pool_out_pk.shape),
                  _const_spec(w_four_out_pk.shape),
                  _const_spec(w_attn_out_pk.shape),
                  _const_spec((1, 3 * D_MODEL)),
                  _const_spec(w_o_pk.shape),
                  _const_spec((1, D_MODEL))],
        out_specs=pl.BlockSpec((None, tile, D_MODEL), lambda bi, i: (bi, i, 0)),
        out_shape=jax.ShapeDtypeStruct((b, seq, D_MODEL), F32),
        scratch_shapes=[pltpu.VMEM((D_MODEL // LANES, tile + 2 * POOL_HALO, LANES), F32),
                        pltpu.VMEM((N_MEM, 2 * ATTN_WIDTH), BF16)],
        compiler_params=pltpu.CompilerParams(dimension_semantics=("arbitrary", "arbitrary"),
                                             vmem_limit_bytes=VMEM_LIMIT),
        name="encoder_main",
    )(x, x, x, f, mem, norm_in, norm_mem, w_in_pk, w_kv_pk, w_pool_grp_pk, pool_scale,
      w_pool_out_pk, w_four_out_pk, w_attn_out_pk, b_gate, w_o_pk, norm_f)


def _trunk(x, mem, norm_in, norm_mem, w_in_pk, w_pool_grp_pk, pool_scale, chan_pk, w_kv_pk,
           w_pool_out_pk, w_four_out_pk, w_attn_out_pk, b_gate, w_o_pk, norm_f):
    b, seq, _ = x.shape
    stage_a, tw, stage_b = _dft_constants(seq)
    stage_a, stage_b = (jnp.asarray(a).astype(BF16) for a in (stage_a, stage_b))
    t = _fourier_a_call(x, norm_in, w_in_pk, chan_pk, stage_a, tw)
    f = _fourier_b_call(t, stage_b, seq)
    f = jnp.swapaxes(f, 1, 2).reshape(b, seq, FOUR_WIDTH)
    return _main_call(x, f, mem, norm_in, norm_mem, w_in_pk, w_kv_pk, w_pool_grp_pk, pool_scale,
                      w_pool_out_pk, w_four_out_pk, w_attn_out_pk, b_gate, w_o_pk, norm_f)


def kernel(x_prompt, x_sample, mem_prompt, mem_sample, norm_in, norm_mem, w_in, w_pool_grp, pool_scale,
           w_four_grp, w_kv, w_pool_out, w_four_out, w_attn_out, b_gate, w_o, norm_f):
    assert norm_in.shape[0] == 1, "single-layer trunk"
    w_in_pk, w_pool_grp_pk, w_kv_pk, w_pool_out_pk, w_four_out_pk, w_attn_out_pk, w_o_pk = _pack_rows(
        w_in[0], w_pool_grp[0], w_kv[0], w_pool_out[0], w_four_out[0], w_attn_out[0], w_o[0])
    shared = (norm_in, norm_mem, w_in_pk, w_pool_grp_pk, pool_scale, _fold_chan_call(w_four_grp[0]), w_kv_pk,
              w_pool_out_pk, w_four_out_pk, w_attn_out_pk, b_gate, w_o_pk, norm_f.reshape(1, D_MODEL))
    return (_trunk(x_prompt, mem_prompt, *shared), _trunk(x_sample, mem_sample, *shared))
```

```python
import functools
import math

import jax
import jax.numpy as jnp
import numpy as np
from jax import lax
from jax.experimental import pallas as pl
from jax.experimental.pallas import tpu as pltpu

D_MODEL = 1024
EPS = 1e-6
POOL_WINDOWS = (2, 4, 8, 16)
POOL_GDIM = 256
POOL_HALO = 8
FOUR_GROUPS = 4
FOUR_GDIM = 128
FOUR_WIDTH = FOUR_GROUPS * FOUR_GDIM
ATTN_HEADS = 4
HEAD_DIM = 128
ATTN_WIDTH = ATTN_HEADS * HEAD_DIM
N_MEM = 256
OFF_PV, OFF_PG, OFF_FV, OFF_FG, OFF_Q, OFF_AG, OFF_MG = 0, 1024, 2048, 2560, 3072, 3584, 4096

SUBLANES = 8
LANES = 128
KRON_DEPTH = 256
FOURIER_A_STEP_TOKENS = 2048
FOURIER_B_STEP_TOKENS = 2048
TOKEN_TILE = 512
PACK_STEPS = 8
VMEM_LIMIT = 56 * 1024 * 1024

BF16 = jnp.bfloat16
F32 = jnp.float32


def _dot(a, b):
    return jnp.dot(a, b, preferred_element_type=F32)


def _pack_kernel(*refs):
    n = len(refs) // 2
    for w_ref, o_ref in zip(refs[:n], refs[n:]):
        o_ref[...] = pltpu.bitcast(w_ref[...].astype(BF16), jnp.uint32)


def _pack_rows(*ws):
    flat = [w.reshape(-1, w.shape[-1]) for w in ws]
    in_specs, out_specs, out_shape = [], [], []
    for w in flat:
        k, n = w.shape
        in_specs.append(pl.BlockSpec((k // PACK_STEPS, n), lambda i: (i, 0)))
        out_specs.append(pl.BlockSpec((k // (2 * PACK_STEPS), n), lambda i: (i, 0)))
        out_shape.append(jax.ShapeDtypeStruct((k // 2, n), jnp.uint32))
    packed = pl.pallas_call(
        _pack_kernel,
        grid=(PACK_STEPS,),
        in_specs=in_specs,
        out_specs=out_specs,
        out_shape=out_shape,
        compiler_params=pltpu.CompilerParams(dimension_semantics=("arbitrary",)),
        name="pack_weights",
    )(*flat)
    return [p.reshape(*w.shape[:-2], w.shape[-2] // 2, w.shape[-1]) for p, w in zip(packed, ws)]


def _unpack_rows(w_u32):
    return pltpu.bitcast(w_u32, BF16)


def _rmsnorm(x, g):
    r = lax.rsqrt(jnp.mean(x * x, axis=-1, keepdims=True) + EPS)
    return x * r * g


def _sigmoid(x):
    return 0.5 * jnp.tanh(0.5 * x) + 0.5


def _silu(x):
    return x * _sigmoid(x)


def _const_spec(shape):
    nd = len(shape)
    return pl.BlockSpec(shape, lambda *_: (0,) * nd, pipeline_mode=pl.Buffered(1))


@functools.cache
def _channel_dft():
    c = np.arange(FOUR_GDIM)
    ang = 2.0 * np.pi * np.outer(c, c) / FOUR_GDIM
    return np.concatenate([np.cos(ang), -np.sin(ang)], axis=0).astype(np.float32)


def _dft_n1(seq):
    return min(KRON_DEPTH // SUBLANES, seq // 128)


@functools.cache
def _dft_constants(seq):
    n1 = _dft_n1(seq)
    n2 = seq // n1
    group = KRON_DEPTH // n1
    k1 = np.arange(n1)
    psi = 2.0 * np.pi * np.outer(k1, k1) / n1
    eye = np.eye(group)
    mr = np.kron(np.cos(psi), eye)
    mi = np.kron(-np.sin(psi), eye)
    stage_a = np.stack([np.concatenate([mr, -mi], axis=1),
                        np.concatenate([mi, mr], axis=1)], axis=0)
    j = np.arange(n2 // group)
    r = np.arange(group)
    s2 = (group * j[:, None, None] + r[None, None, :])
    phi = 2.0 * np.pi * k1[None, :, None] * s2 / seq
    phi = phi.reshape(len(j), KRON_DEPTH, 1)
    tw = np.concatenate([np.broadcast_to(np.cos(phi), phi.shape[:2] + (128,)),
                         np.broadcast_to(-np.sin(phi), phi.shape[:2] + (128,))], axis=2)
    k2 = np.arange(n2)
    th = 2.0 * np.pi * np.outer(k2, k2) / n2
    stage_b = np.concatenate([np.cos(th), np.sin(th)], axis=1)
    return tuple(np.asarray(a, np.float32) for a in (stage_a, tw, stage_b))


def _fold_chan_kernel(c_ref, wg_ref, o_ref):
    for h in range(FOUR_GROUPS):
        r = jnp.dot(c_ref[...], wg_ref[h], precision=lax.Precision.HIGHEST, preferred_element_type=F32)
        folded = jnp.concatenate([r[:FOUR_GDIM], r[FOUR_GDIM:]], axis=1)
        o_ref[h] = pltpu.bitcast(folded.astype(BF16), jnp.uint32)


def _fold_chan_call(w_four_grp):
    return pl.pallas_call(
        _fold_chan_kernel,
        out_shape=jax.ShapeDtypeStruct((FOUR_GROUPS, FOUR_GDIM // 2, 2 * FOUR_GDIM), jnp.uint32),
        name="fold_chan",
    )(jnp.asarray(_channel_dft()), w_four_grp)


def _fourier_a_kernel(x_ref, g_ref, wfv_ref, chan_ref, sta_ref, tw_ref, t_ref, *, n1, row_groups):
    group = KRON_DEPTH // n1
    step_rows = row_groups * group
    x = x_ref[...].reshape(n1 * step_rows, D_MODEL)
    xn = _rmsnorm(x, g_ref[...]).astype(BF16)
    fv = _dot(xn, _unpack_rows(wfv_ref[...])).astype(BF16)
    zr, zi = [], []
    for h in range(FOUR_GROUPS):
        z = _dot(fv[:, h * FOUR_GDIM:(h + 1) * FOUR_GDIM], _unpack_rows(chan_ref[h]))
        zr.append(z[:, :FOUR_GDIM])
        zi.append(z[:, FOUR_GDIM:])
    zr = jnp.concatenate(zr, axis=1).reshape(n1, step_rows, FOUR_WIDTH)
    zi = jnp.concatenate(zi, axis=1).reshape(n1, step_rows, FOUR_WIDTH)
    outs = []
    for g in range(row_groups):
        rs = slice(g * group, (g + 1) * group)
        zcat = jnp.concatenate([zr[:, rs, :].reshape(KRON_DEPTH, FOUR_WIDTH),
                                zi[:, rs, :].reshape(KRON_DEPTH, FOUR_WIDTH)], axis=0).astype(BF16)
        tr = _dot(sta_ref[0], zcat)
        ti = _dot(sta_ref[1], zcat)
        twr = tw_ref[g, :, :FOUR_GDIM]
        twi = tw_ref[g, :, FOUR_GDIM:]
        re, im = [], []
        for h in range(FOUR_GROUPS):
            a, b = tr[:, h * FOUR_GDIM:(h + 1) * FOUR_GDIM], ti[:, h * FOUR_GDIM:(h + 1) * FOUR_GDIM]
            re.append(a * twr - b * twi)
            im.append(a * twi + b * twr)
        outs.append(jnp.concatenate(re + im, axis=1).reshape(n1, group, 2 * FOUR_WIDTH))
    t_ref[...] = jnp.concatenate(outs, axis=1).astype(BF16)


def _fourier_a_call(x, norm_in, w_in_pk, chan, stage_a, tw):
    b, seq, _ = x.shape
    n1 = _dft_n1(seq)
    n2 = seq // n1
    step_rows = FOURIER_A_STEP_TOKENS // n1
    row_groups = step_rows // (KRON_DEPTH // n1)
    x4 = x.reshape(b, n1, n2, D_MODEL)
    return pl.pallas_call(
        functools.partial(_fourier_a_kernel, n1=n1, row_groups=row_groups),
        grid=(b, n2 // step_rows),
        in_specs=[pl.BlockSpec((None, n1, step_rows, D_MODEL), lambda i, j: (i, 0, j, 0)),
                  _const_spec((1, D_MODEL)),
                  pl.BlockSpec((D_MODEL // 2, FOUR_WIDTH), lambda i, j: (0, OFF_FV // FOUR_WIDTH),
                               pipeline_mode=pl.Buffered(1)),
                  _const_spec(chan.shape),
                  _const_spec((2, KRON_DEPTH, 2 * KRON_DEPTH)),
                  pl.BlockSpec((row_groups, KRON_DEPTH, 2 * FOUR_GDIM), lambda i, j: (j, 0, 0))],
        out_specs=pl.BlockSpec((None, n1, step_rows, 2 * FOUR_WIDTH), lambda i, j: (i, 0, j, 0)),
        out_shape=jax.ShapeDtypeStruct((b, n1, n2, 2 * FOUR_WIDTH), BF16),
        compiler_params=pltpu.CompilerParams(dimension_semantics=("arbitrary", "arbitrary")),
        name="fourier_a",
    )(x4, norm_in, w_in_pk, chan, stage_a, tw)


def _fourier_b_kernel(t_ref, stb_ref, f_ref, *, k1_per_step, scale):
    for k in range(k1_per_step):
        t = t_ref[k]
        tcat = jnp.concatenate([t[:, :FOUR_WIDTH], t[:, FOUR_WIDTH:]], axis=0)
        f_ref[k] = (_dot(stb_ref[...], tcat) * scale).astype(BF16)


def _fourier_b_call(t, stage_b, seq):
    b, n1, n2, _ = t.shape
    k1_per_step = max(1, FOURIER_B_STEP_TOKENS // n2)
    kern = functools.partial(_fourier_b_kernel, k1_per_step=k1_per_step,
                             scale=1.0 / math.sqrt(seq * FOUR_GDIM))
    return pl.pallas_call(
        kern,
        grid=(b, n1 // k1_per_step),
        in_specs=[pl.BlockSpec((None, k1_per_step, n2, 2 * FOUR_WIDTH), lambda i, j: (i, j, 0, 0)),
                  _const_spec((n2, 2 * n2))],
        out_specs=pl.BlockSpec((None, k1_per_step, n2, FOUR_WIDTH), lambda i, j: (i, j, 0, 0)),
        out_shape=jax.ShapeDtypeStruct((b, n1, n2, FOUR_WIDTH), BF16),
        compiler_params=pltpu.CompilerParams(dimension_semantics=("arbitrary", "arbitrary")),
        name="fourier_b",
    )(t, stage_b)


def _main_kernel(xm_ref, xp_ref, xq_ref, f_ref, mem_ref, gin_ref, gmem_ref, win_ref, wkv_ref, wpg_ref, psc_ref,
                 wpo_ref, wfo_ref, wao_ref, bg_ref, wo_ref, gf_ref, out_ref, uext_ref, kv_ref, *, seq, tile):
    i = pl.program_id(1)
    last = pl.num_programs(1) - 1

    @pl.when(i == 0)
    def _project_memory():
        mn = _rmsnorm(mem_ref[...], gmem_ref[...]).astype(BF16)
        kv_ref[...] = _dot(mn, _unpack_rows(wkv_ref[...])).astype(BF16)

    gin = gin_ref[...]
    x = xm_ref[...]
    xn = _rmsnorm(x, gin).astype(BF16)

    def proj(off, width, lhs=xn):
        return _dot(lhs, _unpack_rows(win_ref[:, off:off + width]))

    def gate(branch):
        off = branch * D_MODEL
        return _sigmoid(proj(OFF_MG + off, D_MODEL) + bg_ref[:, off:off + D_MODEL])


    halo = jnp.concatenate([xp_ref[...], xq_ref[...]], axis=0)
    pv_ext = proj(OFF_PV, D_MODEL, jnp.concatenate([xn, _rmsnorm(halo, gin).astype(BF16)], axis=0))
    pv = pv_ext[:tile]
    pv_prev = jnp.where(i > 0, pv_ext[tile:tile + POOL_HALO], 0.0)
    pv_next = jnp.where(i < last, pv_ext[tile + POOL_HALO:], 0.0)
    for s in range(D_MODEL // LANES):
        lanes = slice(s * LANES, (s + 1) * LANES)
        uext_ref[s, 0:POOL_HALO, :] = pv_prev[:, lanes]
        uext_ref[s, POOL_HALO:POOL_HALO + tile, :] = pv[:, lanes]
        uext_ref[s, POOL_HALO + tile:, :] = pv_next[:, lanes]

    q = proj(OFF_Q, ATTN_WIDTH).astype(BF16)
    silu_pg = _silu(proj(OFF_PG, D_MODEL))
    gate_a = gate(0)

    t = i * tile + lax.broadcasted_iota(jnp.int32, (tile, 1), 0)
    slabs_per_group = POOL_GDIM // LANES
    ys = []
    for g, w in enumerate(POOL_WINDOWS):
        half = w // 2
        inv_cnt = 1.0 / (jnp.minimum(t + half, seq) - jnp.maximum(t - half, 0)).astype(F32)
        ps = []
        for s in range(g * slabs_per_group, (g + 1) * slabs_per_group):
            win = uext_ref[s, pl.ds(POOL_HALO - half, tile, stride=1), :]
            for jj in range(1, w):
                win = win + uext_ref[s, pl.ds(POOL_HALO - half + jj, tile, stride=1), :]
            ps.append(win * inv_cnt - pv[:, s * LANES:(s + 1) * LANES])
        p = jnp.concatenate(ps, axis=1).astype(BF16)
        ys.append(_dot(p, _unpack_rows(wpg_ref[g])))
    ya_in = (jnp.concatenate(ys, axis=1) * psc_ref[...] * silu_pg).astype(BF16)

    es, ls = [], []
    for h in range(ATTN_HEADS):
        kh = kv_ref[:, h * HEAD_DIM:(h + 1) * HEAD_DIM]
        s = lax.dot_general(q[:, h * HEAD_DIM:(h + 1) * HEAD_DIM], kh, (((1,), (1,)), ((), ())),
                            preferred_element_type=F32) * (1.0 / math.sqrt(HEAD_DIM))
        e = jnp.exp(s - jnp.max(s, axis=-1, keepdims=True))
        ls.append(jnp.sum(e, axis=-1, keepdims=True))
        es.append(e.astype(BF16))

    yb_in = (f_ref[...] * _silu(proj(OFF_FG, FOUR_WIDTH))).astype(BF16)
    gate_b = gate(1)

    ya = _dot(ya_in, _unpack_rows(wpo_ref[...]))
    merged = gate_a * ya

    silu_ag = _silu(proj(OFF_AG, ATTN_WIDTH))
    o = jnp.concatenate(
        [_dot(es[h], kv_ref[:, ATTN_WIDTH + h * HEAD_DIM:ATTN_WIDTH + (h + 1) * HEAD_DIM]) / ls[h]
         for h in range(ATTN_HEADS)], axis=1)
    yc_in = (o * silu_ag).astype(BF16)

    yb = _dot(yb_in, _unpack_rows(wfo_ref[...]))
    merged = merged + gate_b * yb
    gate_c = gate(2)
    yc = _dot(yc_in, _unpack_rows(wao_ref[...]))
    merged = merged + gate_c * yc

    hres = x + _dot(merged.astype(BF16), _unpack_rows(wo_ref[...]))
    out_ref[...] = _rmsnorm(hres, gf_ref[...])


def _main_call(x, f, mem, norm_in, norm_mem, w_in_pk, w_kv_pk, w_pool_grp_pk, pool_scale,
               w_pool_out_pk, w_four_out_pk, w_attn_out_pk, b_gate, w_o_pk, norm_f):
    b, seq, _ = x.shape
    tile = TOKEN_TILE
    nt = seq // tile
    hb = tile // POOL_HALO
    nhb = seq // POOL_HALO
    kern = functools.partial(_main_kernel, seq=seq, tile=tile)
    return pl.pallas_call(
        kern,
        grid=(b, nt),
        in_specs=[pl.BlockSpec((None, tile, D_MODEL), lambda bi, i: (bi, i, 0)),
                  pl.BlockSpec((None, POOL_HALO, D_MODEL), lambda bi, i: (bi, jnp.maximum(i * hb - 1, 0), 0)),
                  pl.BlockSpec((None, POOL_HALO, D_MODEL),
                               lambda bi, i: (bi, jnp.minimum((i + 1) * hb, nhb - 1), 0)),
                  pl.BlockSpec((None, tile, FOUR_WIDTH), lambda bi, i: (bi, i, 0)),
                  pl.BlockSpec((None, N_MEM, D_MODEL), lambda bi, i: (bi, 0, 0)),
                  _const_spec((1, D_MODEL)),
                  _const_spec((1, D_MODEL)),
                  _const_spec(w_in_pk.shape),
                  _const_spec(w_kv_pk.shape),
                  _const_spec(w_pool_grp_pk.shape),
                  _const_spec((1, D_MODEL)),
                  _const_spec(w_pool_out_pk.shape),
                  _const_spec(w_four_out_pk.shape),
                  _const_spec(w_attn_out_pk.shape),
                  _const_spec((1, 3 * D_MODEL)),
                  _const_spec(w_o_pk.shape),
                  _const_spec((1, D_MODEL))],
        out_specs=pl.BlockSpec((None, tile, D_MODEL), lambda bi, i: (bi, i, 0)),
        out_shape=jax.ShapeDtypeStruct((b, seq, D_MODEL), F32),
        scratch_shapes=[pltpu.VMEM((D_MODEL // LANES, tile + 2 * POOL_HALO, LANES), F32),
                        pltpu.VMEM((N_MEM, 2 * ATTN_WIDTH), BF16)],
        compiler_params=pltpu.CompilerParams(dimension_semantics=("arbitrary", "arbitrary"),
                                             vmem_limit_bytes=VMEM_LIMIT),
        name="encoder_main",
    )(x, x, x, f, mem, norm_in, norm_mem, w_in_pk, w_kv_pk, w_pool_grp_pk, pool_scale,
      w_pool_out_pk, w_four_out_pk, w_attn_out_pk, b_gate, w_o_pk, norm_f)


def _trunk(x, mem, norm_in, norm_mem, w_in_pk, w_pool_grp_pk, pool_scale, chan_pk, w_kv_pk,
           w_pool_out_pk, w_four_out_pk, w_attn_out_pk, b_gate, w_o_pk, norm_f):
    b, seq, _ = x.shape
    stage_a, tw, stage_b = _dft_constants(seq)
    stage_a, stage_b = (jnp.asarray(a).astype(BF16) for a in (stage_a, stage_b))
    t = _fourier_a_call(x, norm_in, w_in_pk, chan_pk, stage_a, tw)
    f = _fourier_b_call(t, stage_b, seq)
    f = jnp.swapaxes(f, 1, 2).reshape(b, seq, FOUR_WIDTH)
    return _main_call(x, f, mem, norm_in, norm_mem, w_in_pk, w_kv_pk, w_pool_grp_pk, pool_scale,
                      w_pool_out_pk, w_four_out_pk, w_attn_out_pk, b_gate, w_o_pk, norm_f)


def kernel(x_prompt, x_sample, mem_prompt, mem_sample, norm_in, norm_mem, w_in, w_pool_grp, pool_scale,
           w_four_grp, w_kv, w_pool_out, w_four_out, w_attn_out, b_gate, w_o, norm_f):
    assert norm_in.shape[0] == 1, "single-layer trunk"
    w_in_pk, w_pool_grp_pk, w_kv_pk, w_pool_out_pk, w_four_out_pk, w_attn_out_pk, w_o_pk = _pack_rows(
        w_in[0], w_pool_grp[0], w_kv[0], w_pool_out[0], w_four_out[0], w_attn_out[0], w_o[0])
    shared = (norm_in, norm_mem, w_in_pk, w_pool_grp_pk, pool_scale, _fold_chan_call(w_four_grp[0]), w_kv_pk,
              w_pool_out_pk, w_four_out_pk, w_attn_out_pk, b_gate, w_o_pk, norm_f.reshape(1, D_MODEL))
    return (_trunk(x_prompt, mem_prompt, *shared), _trunk(x_sample, mem_sample, *shared))
```

```python
import functools
import math

import jax
import jax.numpy as jnp
import numpy as np
from jax import lax
from jax.experimental import pallas as pl
from jax.experimental.pallas import tpu as pltpu

D_MODEL = 1024
EPS = 1e-6
POOL_WINDOWS = (2, 4, 8, 16)
POOL_GDIM = 256
POOL_HALO = 8
FOUR_GROUPS = 4
FOUR_GDIM = 128
FOUR_WIDTH = FOUR_GROUPS * FOUR_GDIM
ATTN_HEADS = 4
HEAD_DIM = 128
ATTN_WIDTH = ATTN_HEADS * HEAD_DIM
N_MEM = 256
OFF_PV, OFF_PG, OFF_FV, OFF_FG, OFF_Q, OFF_AG, OFF_MG = 0, 1024, 2048, 2560, 3072, 3584, 4096

SUBLANES = 8
LANES = 128
KRON_DEPTH = 256
FOURIER_A_STEP_TOKENS = 2048
FOURIER_B_STEP_TOKENS = 2048
TOKEN_TILE = 512
PACK_STEPS = 8
KV_BATCHES_PER_STEP = 4
VMEM_LIMIT = 56 * 1024 * 1024

BF16 = jnp.bfloat16
F32 = jnp.float32


def _dot(a, b):
    return jnp.dot(a, b, preferred_element_type=F32)


def _pack_kernel(*refs):
    n = len(refs) // 2
    for w_ref, o_ref in zip(refs[:n], refs[n:]):
        o_ref[...] = pltpu.bitcast(w_ref[...].astype(BF16), jnp.uint32)


def _pack_rows(*ws):
    flat = [w.reshape(-1, w.shape[-1]) for w in ws]
    in_specs, out_specs, out_shape = [], [], []
    for w in flat:
        k, n = w.shape
        in_specs.append(pl.BlockSpec((k // PACK_STEPS, n), lambda i: (i, 0)))
        out_specs.append(pl.BlockSpec((k // (2 * PACK_STEPS), n), lambda i: (i, 0)))
        out_shape.append(jax.ShapeDtypeStruct((k // 2, n), jnp.uint32))
    packed = pl.pallas_call(
        _pack_kernel,
        grid=(PACK_STEPS,),
        in_specs=in_specs,
        out_specs=out_specs,
        out_shape=out_shape,
        compiler_params=pltpu.CompilerParams(dimension_semantics=("arbitrary",)),
        name="pack_weights",
    )(*flat)
    return [p.reshape(*w.shape[:-2], w.shape[-2] // 2, w.shape[-1]) for p, w in zip(packed, ws)]


def _unpack_rows(w_u32):
    return pltpu.bitcast(w_u32, BF16)


def _rmsnorm(x, g):
    r = lax.rsqrt(jnp.mean(x * x, axis=-1, keepdims=True) + EPS)
    return x * r * g


def _sigmoid(x):
    return 0.5 * jnp.tanh(0.5 * x) + 0.5


def _silu(x):
    return x * _sigmoid(x)


def _const_spec(shape):
    nd = len(shape)
    return pl.BlockSpec(shape, lambda *_: (0,) * nd, pipeline_mode=pl.Buffered(1))


@functools.cache
def _channel_dft():
    c = np.arange(FOUR_GDIM)
    ang = 2.0 * np.pi * np.outer(c, c) / FOUR_GDIM
    return np.concatenate([np.cos(ang), -np.sin(ang)], axis=0).astype(np.float32)


def _dft_n1(seq):
    return min(KRON_DEPTH // SUBLANES, seq // (KRON_DEPTH // 2))


@functools.cache
def _dft_constants(seq):
    n1 = _dft_n1(seq)
    n2 = seq // n1
    group = KRON_DEPTH // n1
    k1 = np.arange(n1)
    psi = 2.0 * np.pi * np.outer(k1, k1) / n1
    eye = np.eye(group)
    mr = np.kron(np.cos(psi), eye)
    mi = np.kron(-np.sin(psi), eye)
    stage_a = np.stack([np.concatenate([mr, -mi], axis=1),
                        np.concatenate([mi, mr], axis=1)], axis=0)
    j = np.arange(n2 // group)
    r = np.arange(group)
    s2 = (group * j[:, None, None] + r[None, None, :])
    phi = 2.0 * np.pi * k1[None, :, None] * s2 / seq
    phi = phi.reshape(len(j), KRON_DEPTH, 1)
    tw = np.concatenate([np.broadcast_to(np.cos(phi), phi.shape[:2] + (LANES,)),
                         np.broadcast_to(-np.sin(phi), phi.shape[:2] + (LANES,))], axis=2)
    k2 = np.arange(n2)
    th = 2.0 * np.pi * np.outer(k2, k2) / n2
    stage_b = np.concatenate([np.cos(th), np.sin(th)], axis=1)
    return tuple(np.asarray(a, np.float32) for a in (stage_a, tw, stage_b))


def _fold_chan_kernel(c_ref, wg_ref, o_ref):
    for h in range(FOUR_GROUPS):
        r = jnp.dot(c_ref[...], wg_ref[h], precision=lax.Precision.HIGHEST, preferred_element_type=F32)
        folded = jnp.concatenate([r[:FOUR_GDIM], r[FOUR_GDIM:]], axis=1)
        o_ref[h] = pltpu.bitcast(folded.astype(BF16), jnp.uint32)


def _fold_chan_call(w_four_grp):
    return pl.pallas_call(
        _fold_chan_kernel,
        out_shape=jax.ShapeDtypeStruct((FOUR_GROUPS, FOUR_GDIM // 2, 2 * FOUR_GDIM), jnp.uint32),
        name="fold_chan",
    )(jnp.asarray(_channel_dft()), w_four_grp)


def _kv_kernel(mem_ref, g_ref, wkv_ref, kv_ref):
    nb = mem_ref.shape[0]
    mn = _rmsnorm(mem_ref[...].reshape(nb * N_MEM, D_MODEL), g_ref[...]).astype(BF16)
    kv_ref[...] = _dot(mn, _unpack_rows(wkv_ref[...])).astype(BF16).reshape(nb, N_MEM, 2 * ATTN_WIDTH)


def _kv_call(mem, norm_mem, w_kv_pk):
    b = mem.shape[0]
    per_step = math.gcd(b, KV_BATCHES_PER_STEP)
    return pl.pallas_call(
        _kv_kernel,
        grid=(b // per_step,),
        in_specs=[pl.BlockSpec((per_step, N_MEM, D_MODEL), lambda i: (i, 0, 0)),
                  _const_spec((1, D_MODEL)),
                  _const_spec(w_kv_pk.shape)],
        out_specs=pl.BlockSpec((per_step, N_MEM, 2 * ATTN_WIDTH), lambda i: (i, 0, 0)),
        out_shape=jax.ShapeDtypeStruct((b, N_MEM, 2 * ATTN_WIDTH), BF16),
        compiler_params=pltpu.CompilerParams(dimension_semantics=("arbitrary",)),
        name="kv_proj",
    )(mem, norm_mem, w_kv_pk)


def _fourier_a_kernel(x_ref, g_ref, wfv_ref, chan_ref, sta_ref, tw_ref, t_ref, *, n1, row_groups):
    group = KRON_DEPTH // n1
    step_rows = row_groups * group
    x = x_ref[...].reshape(n1 * step_rows, D_MODEL)
    xn = _rmsnorm(x, g_ref[...]).astype(BF16)
    fv = _dot(xn, _unpack_rows(wfv_ref[...])).astype(BF16)
    zr, zi = [], []
    for h in range(FOUR_GROUPS):
        z = _dot(fv[:, h * FOUR_GDIM:(h + 1) * FOUR_GDIM], _unpack_rows(chan_ref[h]))
        zr.append(z[:, :FOUR_GDIM])
        zi.append(z[:, FOUR_GDIM:])
    zr = jnp.concatenate(zr, axis=1).reshape(n1, step_rows, FOUR_WIDTH)
    zi = jnp.concatenate(zi, axis=1).reshape(n1, step_rows, FOUR_WIDTH)
    outs = []
    for g in range(row_groups):
        rs = slice(g * group, (g + 1) * group)
        zcat = jnp.concatenate([zr[:, rs, :].reshape(KRON_DEPTH, FOUR_WIDTH),
                                zi[:, rs, :].reshape(KRON_DEPTH, FOUR_WIDTH)], axis=0).astype(BF16)
        tr = _dot(sta_ref[0], zcat)
        ti = _dot(sta_ref[1], zcat)
        twr = tw_ref[g, :, :FOUR_GDIM]
        twi = tw_ref[g, :, FOUR_GDIM:]
        re, im = [], []
        for h in range(FOUR_GROUPS):
            a, b = tr[:, h * FOUR_GDIM:(h + 1) * FOUR_GDIM], ti[:, h * FOUR_GDIM:(h + 1) * FOUR_GDIM]
            re.append(a * twr - b * twi)
            im.append(a * twi + b * twr)
        outs.append(jnp.concatenate(re + im, axis=1).reshape(n1, group, 2 * FOUR_WIDTH))
    t_ref[...] = jnp.concatenate(outs, axis=1).astype(BF16)


def _fourier_a_call(x, norm_in, w_in_pk, chan, stage_a, tw):
    b, seq, _ = x.shape
    n1 = _dft_n1(seq)
    n2 = seq // n1
    step_rows = FOURIER_A_STEP_TOKENS // n1
    row_groups = step_rows // (KRON_DEPTH // n1)
    x4 = x.reshape(b, n1, n2, D_MODEL)
    return pl.pallas_call(
        functools.partial(_fourier_a_kernel, n1=n1, row_groups=row_groups),
        grid=(b, n2 // step_rows),
        in_specs=[pl.BlockSpec((None, n1, step_rows, D_MODEL), lambda i, j: (i, 0, j, 0)),
                  _const_spec((1, D_MODEL)),
                  pl.BlockSpec((D_MODEL // 2, FOUR_WIDTH), lambda i, j: (0, OFF_FV // FOUR_WIDTH),
                               pipeline_mode=pl.Buffered(1)),
                  _const_spec(chan.shape),
                  _const_spec((2, KRON_DEPTH, 2 * KRON_DEPTH)),
                  pl.BlockSpec((row_groups, KRON_DEPTH, 2 * FOUR_GDIM), lambda i, j: (j, 0, 0))],
        out_specs=pl.BlockSpec((None, n1, step_rows, 2 * FOUR_WIDTH), lambda i, j: (i, 0, j, 0)),
        out_shape=jax.ShapeDtypeStruct((b, n1, n2, 2 * FOUR_WIDTH), BF16),
        compiler_params=pltpu.CompilerParams(dimension_semantics=("arbitrary", "arbitrary")),
        name="fourier_a",
    )(x4, norm_in, w_in_pk, chan, stage_a, tw)


def _fourier_b_kernel(t_ref, stb_ref, f_ref, *, k1_per_step, scale):
    for k in range(k1_per_step):
        t = t_ref[k]
        tcat = jnp.concatenate([t[:, :FOUR_WIDTH], t[:, FOUR_WIDTH:]], axis=0)
        f_ref[k] = (_dot(stb_ref[...], tcat) * scale).astype(BF16)


def _fourier_b_call(t, stage_b, seq):
    b, n1, n2, _ = t.shape
    k1_per_step = max(1, FOURIER_B_STEP_TOKENS // n2)
    kern = functools.partial(_fourier_b_kernel, k1_per_step=k1_per_step,
                             scale=1.0 / math.sqrt(seq * FOUR_GDIM))
    return pl.pallas_call(
        kern,
        grid=(b, n1 // k1_per_step),
        in_specs=[pl.BlockSpec((None, k1_per_step, n2, 2 * FOUR_WIDTH), lambda i, j: (i, j, 0, 0)),
                  _const_spec((n2, 2 * n2))],
        out_specs=pl.BlockSpec((None, k1_per_step, n2, FOUR_WIDTH), lambda i, j: (i, j, 0, 0)),
        out_shape=jax.ShapeDtypeStruct((b, n1, n2, FOUR_WIDTH), BF16),
        compiler_params=pltpu.CompilerParams(dimension_semantics=("arbitrary", "arbitrary")),
        name="fourier_b",
    )(t, stage_b)


def _main_kernel(xm_ref, xp_ref, xq_ref, f_ref, kv_ref, gin_ref, win_ref, wpg_ref, psc_ref,
                 wpo_ref, wfo_ref, wao_ref, bg_ref, wo_ref, gf_ref, out_ref, uext_ref, *, seq, tile):
    i = pl.program_id(1)
    last = pl.num_programs(1) - 1
    gin = gin_ref[...]
    x = xm_ref[...]
    xn = _rmsnorm(x, gin).astype(BF16)

    def proj(off, width, lhs=xn):
        return _dot(lhs, _unpack_rows(win_ref[:, off:off + width]))

    def gate(branch):
        off = branch * D_MODEL
        return _sigmoid(proj(OFF_MG + off, D_MODEL) + bg_ref[:, off:off + D_MODEL])


    halo = jnp.concatenate([xp_ref[...], xq_ref[...]], axis=0)
    pv_ext = proj(OFF_PV, D_MODEL, jnp.concatenate([xn, _rmsnorm(halo, gin).astype(BF16)], axis=0))
    pv = pv_ext[:tile]
    pv_prev = jnp.where(i > 0, pv_ext[tile:tile + POOL_HALO], 0.0)
    pv_next = jnp.where(i < last, pv_ext[tile + POOL_HALO:], 0.0)
    for s in range(D_MODEL // LANES):
        lanes = slice(s * LANES, (s + 1) * LANES)
        uext_ref[s, 0:POOL_HALO, :] = pv_prev[:, lanes]
        uext_ref[s, POOL_HALO:POOL_HALO + tile, :] = pv[:, lanes]
        uext_ref[s, POOL_HALO + tile:, :] = pv_next[:, lanes]

    q = proj(OFF_Q, ATTN_WIDTH).astype(BF16)
    silu_pg = _silu(proj(OFF_PG, D_MODEL))
    gate_a = gate(0)

    t = i * tile + lax.broadcasted_iota(jnp.int32, (tile, 1), 0)
    slabs_per_group = POOL_GDIM // LANES
    ys = []
    for g, w in enumerate(POOL_WINDOWS):
        half = w // 2
        inv_cnt = 1.0 / (jnp.minimum(t + half, seq) - jnp.maximum(t - half, 0)).astype(F32)
        ps = []
        for s in range(g * slabs_per_group, (g + 1) * slabs_per_group):
            win = uext_ref[s, pl.ds(POOL_HALO - half, tile, stride=1), :]
            for jj in range(1, w):
                win = win + uext_ref[s, pl.ds(POOL_HALO - half + jj, tile, stride=1), :]
            ps.append(win * inv_cnt - pv[:, s * LANES:(s + 1) * LANES])
        p = jnp.concatenate(ps, axis=1).astype(BF16)
        ys.append(_dot(p, _unpack_rows(wpg_ref[g])))
    ya_in = (jnp.concatenate(ys, axis=1) * psc_ref[...] * silu_pg).astype(BF16)

    es, ls = [], []
    for h in range(ATTN_HEADS):
        kh = kv_ref[:, h * HEAD_DIM:(h + 1) * HEAD_DIM]
        s = lax.dot_general(q[:, h * HEAD_DIM:(h + 1) * HEAD_DIM], kh, (((1,), (1,)), ((), ())),
                            preferred_element_type=F32) * (1.0 / math.sqrt(HEAD_DIM))
        e = jnp.exp(s - jnp.max(s, axis=-1, keepdims=True))
        ls.append(jnp.sum(e, axis=-1, keepdims=True))
        es.append(e.astype(BF16))

    yb_in = (f_ref[...] * _silu(proj(OFF_FG, FOUR_WIDTH))).astype(BF16)
    gate_b = gate(1)

    ya = _dot(ya_in, _unpack_rows(wpo_ref[...]))
    merged = gate_a * ya

    silu_ag = _silu(proj(OFF_AG, ATTN_WIDTH))
    o = jnp.concatenate(
        [_dot(es[h], kv_ref[:, ATTN_WIDTH + h * HEAD_DIM:ATTN_WIDTH + (h + 1) * HEAD_DIM]) / ls[h]
         for h in range(ATTN_HEADS)], axis=1)
    yc_in = (o * silu_ag).astype(BF16)

    yb = _dot(yb_in, _unpack_rows(wfo_ref[...]))
    merged = merged + gate_b * yb
    gate_c = gate(2)
    yc = _dot(yc_in, _unpack_rows(wao_ref[...]))
    merged = merged + gate_c * yc

    hres = x + _dot(merged.astype(BF16), _unpack_rows(wo_ref[...]))
    out_ref[...] = _rmsnorm(hres, gf_ref[...])


def _main_call(x, f, kv, norm_in, w_in_pk, w_pool_grp_pk, pool_scale,
               w_pool_out_pk, w_four_out_pk, w_attn_out_pk, b_gate, w_o_pk, norm_f):
    b, seq, _ = x.shape
    tile = TOKEN_TILE
    nt = seq // tile
    hb = tile // POOL_HALO
    nhb = seq // POOL_HALO
    kern = functools.partial(_main_kernel, seq=seq, tile=tile)
    return pl.pallas_call(
        kern,
        grid=(b, nt),
        in_specs=[pl.BlockSpec((None, tile, D_MODEL), lambda bi, i: (bi, i, 0)),
                  pl.BlockSpec((None, POOL_HALO, D_MODEL), lambda bi, i: (bi, jnp.maximum(i * hb - 1, 0), 0)),
                  pl.BlockSpec((None, POOL_HALO, D_MODEL),
                               lambda bi, i: (bi, jnp.minimum((i + 1) * hb, nhb - 1), 0)),
                  pl.BlockSpec((None, tile, FOUR_WIDTH), lambda bi, i: (bi, i, 0)),
                  pl.BlockSpec((None, N_MEM, 2 * ATTN_WIDTH), lambda bi, i: (bi, 0, 0)),
                  _const_spec((1, D_MODEL)),
                  _const_spec(w_in_pk.shape),
                  _const_spec(w_pool_grp_pk.shape),
                  _const_spec((1, D_MODEL)),
                  _const_spec(w_pool_out_pk.shape),
                  _const_spec(w_four_out_pk.shape),
                  _const_spec(w_attn_out_pk.shape),
                  _const_spec((1, 3 * D_MODEL)),
                  _const_spec(w_o_pk.shape),
                  _const_spec((1, D_MODEL))],
        out_specs=pl.BlockSpec((None, tile, D_MODEL), lambda bi, i: (bi, i, 0)),
        out_shape=jax.ShapeDtypeStruct((b, seq, D_MODEL), F32),
        scratch_shapes=[pltpu.VMEM((D_MODEL // LANES, tile + 2 * POOL_HALO, LANES), F32)],
        compiler_params=pltpu.CompilerParams(dimension_semantics=("arbitrary", "arbitrary"),
                                             vmem_limit_bytes=VMEM_LIMIT),
        name="encoder_main",
    )(x, x, x, f, kv, norm_in, w_in_pk, w_pool_grp_pk, pool_scale,
      w_pool_out_pk, w_four_out_pk, w_attn_out_pk, b_gate, w_o_pk, norm_f)


def _trunk(x, mem, norm_in, norm_mem, w_in_pk, w_pool_grp_pk, pool_scale, chan_pk, w_kv_pk,
           w_pool_out_pk, w_four_out_pk, w_attn_out_pk, b_gate, w_o_pk, norm_f):
    b, seq, _ = x.shape
    stage_a, tw, stage_b = _dft_constants(seq)
    stage_a, stage_b = (jnp.asarray(a).astype(BF16) for a in (stage_a, stage_b))
    kv = _kv_call(mem, norm_mem, w_kv_pk)
    t = _fourier_a_call(x, norm_in, w_in_pk, chan_pk, stage_a, tw)
    f = _fourier_b_call(t, stage_b, seq)
    f = jnp.swapaxes(f, 1, 2).reshape(b, seq, FOUR_WIDTH)
    return _main_call(x, f, kv, norm_in, w_in_pk, w_pool_grp_pk, pool_scale,
                      w_pool_out_pk, w_four_out_pk, w_attn_out_pk, b_gate, w_o_pk, norm_f)


def kernel(x_prompt, x_sample, mem_prompt, mem_sample, norm_in, norm_mem, w_in, w_pool_grp, pool_scale,
           w_four_grp, w_kv, w_pool_out, w_four_out, w_attn_out, b_gate, w_o, norm_f):
    assert norm_in.shape[0] == 1, "single-layer trunk"
    w_in_pk, w_pool_grp_pk, w_kv_pk, w_pool_out_pk, w_four_out_pk, w_attn_out_pk, w_o_pk = _pack_rows(
        w_in[0], w_pool_grp[0], w_kv[0], w_pool_out[0], w_four_out[0], w_attn_out[0], w_o[0])
    shared = (norm_in, norm_mem, w_in_pk, w_pool_grp_pk, pool_scale, _fold_chan_call(w_four_grp[0]), w_kv_pk,
              w_pool_out_pk, w_four_out_pk, w_attn_out_pk, b_gate, w_o_pk, norm_f.reshape(1, D_MODEL))
    return (_trunk(x_prompt, mem_prompt, *shared), _trunk(x_sample, mem_sample, *shared))
```

```python
import functools
import math

import jax
import jax.numpy as jnp
import numpy as np
from jax import lax
from jax.experimental import pallas as pl
from jax.experimental.pallas import tpu as pltpu

D_MODEL = 1024
EPS = 1e-6
POOL_WINDOWS = (2, 4, 8, 16)
POOL_GDIM = 256
POOL_HALO = 8
FOUR_GROUPS = 4
FOUR_GDIM = 128
FOUR_WIDTH = FOUR_GROUPS * FOUR_GDIM
ATTN_HEADS = 4
HEAD_DIM = 128
ATTN_WIDTH = ATTN_HEADS * HEAD_DIM
N_MEM = 256
OFF_PV, OFF_PG, OFF_FV, OFF_FG, OFF_Q, OFF_AG, OFF_MG = 0, 1024, 2048, 2560, 3072, 3584, 4096

SUBLANES = 8
LANES = 128
KRON_DEPTH = 256
FOURIER_A_STEP_TOKENS = 2048
FOURIER_B_STEP_TOKENS = 2048
TOKEN_TILE = 512
PACK_STEPS = 8
KV_BATCHES_PER_STEP = 4
VMEM_LIMIT = 56 * 1024 * 1024

BF16 = jnp.bfloat16
F32 = jnp.float32


def _dot(a, b):
    return jnp.dot(a, b, preferred_element_type=F32)


def _pack_kernel(*refs):
    n = len(refs) // 2
    for w_ref, o_ref in zip(refs[:n], refs[n:]):
        o_ref[...] = pltpu.bitcast(w_ref[...].astype(BF16), jnp.uint32)


def _pack_rows(*ws):
    flat = [w.reshape(-1, w.shape[-1]) for w in ws]
    in_specs, out_specs, out_shape = [], [], []
    for w in flat:
        k, n = w.shape
        in_specs.append(pl.BlockSpec((k // PACK_STEPS, n), lambda i: (i, 0)))
        out_specs.append(pl.BlockSpec((k // (2 * PACK_STEPS), n), lambda i: (i, 0)))
        out_shape.append(jax.ShapeDtypeStruct((k // 2, n), jnp.uint32))
    packed = pl.pallas_call(
        _pack_kernel,
        grid=(PACK_STEPS,),
        in_specs=in_specs,
        out_specs=out_specs,
        out_shape=out_shape,
        compiler_params=pltpu.CompilerParams(dimension_semantics=("arbitrary",)),
        name="pack_weights",
    )(*flat)
    return [p.reshape(*w.shape[:-2], w.shape[-2] // 2, w.shape[-1]) for p, w in zip(packed, ws)]


def _unpack_rows(w_u32):
    return pltpu.bitcast(w_u32, BF16)


def _rmsnorm(x, g):
    r = lax.rsqrt(jnp.mean(x * x, axis=-1, keepdims=True) + EPS)
    return x * r * g


def _sigmoid(x):
    return 0.5 * jnp.tanh(0.5 * x) + 0.5


def _silu(x):
    return x * _sigmoid(x)


def _const_spec(shape):
    nd = len(shape)
    return pl.BlockSpec(shape, lambda *_: (0,) * nd, pipeline_mode=pl.Buffered(1))


@functools.cache
def _channel_dft():
    c = np.arange(FOUR_GDIM)
    ang = 2.0 * np.pi * np.outer(c, c) / FOUR_GDIM
    return np.concatenate([np.cos(ang), -np.sin(ang)], axis=0).astype(np.float32)


def _dft_n1(seq):
    return min(KRON_DEPTH // SUBLANES, seq // (KRON_DEPTH // 2))


@functools.cache
def _dft_constants(seq):
    n1 = _dft_n1(seq)
    n2 = seq // n1
    group = KRON_DEPTH // n1
    k1 = np.arange(n1)
    psi = 2.0 * np.pi * np.outer(k1, k1) / n1
    eye = np.eye(group)
    mr = np.kron(np.cos(psi), eye)
    mi = np.kron(-np.sin(psi), eye)
    stage_a = np.stack([np.concatenate([mr, -mi], axis=1),
                        np.concatenate([mi, mr], axis=1)], axis=0)
    j = np.arange(n2 // group)
    r = np.arange(group)
    s2 = (group * j[:, None, None] + r[None, None, :])
    phi = 2.0 * np.pi * k1[None, :, None] * s2 / seq
    phi = phi.reshape(len(j), KRON_DEPTH, 1)
    tw = np.concatenate([np.broadcast_to(np.cos(phi), phi.shape[:2] + (LANES,)),
                         np.broadcast_to(-np.sin(phi), phi.shape[:2] + (LANES,))], axis=2)
    k2 = np.arange(n2)
    th = 2.0 * np.pi * np.outer(k2, k2) / n2
    stage_b = np.concatenate([np.cos(th), np.sin(th)], axis=1)
    return tuple(np.asarray(a, np.float32) for a in (stage_a, tw, stage_b))


def _fold_chan_kernel(c_ref, wg_ref, o_ref):
    for h in range(FOUR_GROUPS):
        r = jnp.dot(c_ref[...], wg_ref[h], precision=lax.Precision.HIGHEST, preferred_element_type=F32)
        folded = jnp.concatenate([r[:FOUR_GDIM], r[FOUR_GDIM:]], axis=1)
        o_ref[h] = pltpu.bitcast(folded.astype(BF16), jnp.uint32)


def _fold_chan_call(w_four_grp):
    return pl.pallas_call(
        _fold_chan_kernel,
        out_shape=jax.ShapeDtypeStruct((FOUR_GROUPS, FOUR_GDIM // 2, 2 * FOUR_GDIM), jnp.uint32),
        name="fold_chan",
    )(jnp.asarray(_channel_dft()), w_four_grp)


def _kv_kernel(mem_ref, g_ref, wkv_ref, kv_ref):
    nb = mem_ref.shape[0]
    mn = _rmsnorm(mem_ref[...].reshape(nb * N_MEM, D_MODEL), g_ref[...]).astype(BF16)
    kv_ref[...] = _dot(mn, _unpack_rows(wkv_ref[...])).astype(BF16).reshape(nb, N_MEM, 2 * ATTN_WIDTH)


def _kv_call(mem, norm_mem, w_kv_pk):
    b = mem.shape[0]
    per_step = math.gcd(b, KV_BATCHES_PER_STEP)
    return pl.pallas_call(
        _kv_kernel,
        grid=(b // per_step,),
        in_specs=[pl.BlockSpec((per_step, N_MEM, D_MODEL), lambda i: (i, 0, 0)),
                  _const_spec((1, D_MODEL)),
                  _const_spec(w_kv_pk.shape)],
        out_specs=pl.BlockSpec((per_step, N_MEM, 2 * ATTN_WIDTH), lambda i: (i, 0, 0)),
        out_shape=jax.ShapeDtypeStruct((b, N_MEM, 2 * ATTN_WIDTH), BF16),
        compiler_params=pltpu.CompilerParams(dimension_semantics=("arbitrary",)),
        name="kv_proj",
    )(mem, norm_mem, w_kv_pk)


def _fourier_a_kernel(x_ref, g_ref, wfv_ref, chan_ref, sta_ref, tw_ref, t_ref, *, n1, row_groups):
    group = KRON_DEPTH // n1
    step_rows = row_groups * group
    x = x_ref[...].reshape(n1 * step_rows, D_MODEL)
    xn = _rmsnorm(x, g_ref[...]).astype(BF16)
    fv = _dot(xn, _unpack_rows(wfv_ref[...])).astype(BF16)
    zr, zi = [], []
    for h in range(FOUR_GROUPS):
        z = _dot(fv[:, h * FOUR_GDIM:(h + 1) * FOUR_GDIM], _unpack_rows(chan_ref[h]))
        zr.append(z[:, :FOUR_GDIM])
        zi.append(z[:, FOUR_GDIM:])
    zr = jnp.concatenate(zr, axis=1).reshape(n1, step_rows, FOUR_WIDTH)
    zi = jnp.concatenate(zi, axis=1).reshape(n1, step_rows, FOUR_WIDTH)
    outs = []
    for g in range(row_groups):
        rs = slice(g * group, (g + 1) * group)
        zcat = jnp.concatenate([zr[:, rs, :].reshape(KRON_DEPTH, FOUR_WIDTH),
                                zi[:, rs, :].reshape(KRON_DEPTH, FOUR_WIDTH)], axis=0).astype(BF16)
        tr = _dot(sta_ref[0], zcat)
        ti = _dot(sta_ref[1], zcat)
        twr = tw_ref[g, :, :FOUR_GDIM]
        twi = tw_ref[g, :, FOUR_GDIM:]
        re, im = [], []
        for h in range(FOUR_GROUPS):
            a, b = tr[:, h * FOUR_GDIM:(h + 1) * FOUR_GDIM], ti[:, h * FOUR_GDIM:(h + 1) * FOUR_GDIM]
            re.append(a * twr - b * twi)
            im.append(a * twi + b * twr)
        outs.append(jnp.concatenate(re + im, axis=1).reshape(n1, group, 2 * FOUR_WIDTH))
    t_ref[...] = jnp.concatenate(outs, axis=1).astype(BF16)


def _fourier_a_call(x, norm_in, w_in_pk, chan, stage_a, tw):
    b, seq, _ = x.shape
    n1 = _dft_n1(seq)
    n2 = seq // n1
    step_rows = FOURIER_A_STEP_TOKENS // n1
    row_groups = step_rows // (KRON_DEPTH // n1)
    x4 = x.reshape(b, n1, n2, D_MODEL)
    return pl.pallas_call(
        functools.partial(_fourier_a_kernel, n1=n1, row_groups=row_groups),
        grid=(b, n2 // step_rows),
        in_specs=[pl.BlockSpec((None, n1, step_rows, D_MODEL), lambda i, j: (i, 0, j, 0)),
                  _const_spec((1, D_MODEL)),
                  pl.BlockSpec((D_MODEL // 2, FOUR_WIDTH), lambda i, j: (0, OFF_FV // FOUR_WIDTH),
                               pipeline_mode=pl.Buffered(1)),
                  _const_spec(chan.shape),
                  _const_spec((2, KRON_DEPTH, 2 * KRON_DEPTH)),
                  pl.BlockSpec((row_groups, KRON_DEPTH, 2 * FOUR_GDIM), lambda i, j: (j, 0, 0))],
        out_specs=pl.BlockSpec((None, n1, step_rows, 2 * FOUR_WIDTH), lambda i, j: (i, 0, j, 0)),
        out_shape=jax.ShapeDtypeStruct((b, n1, n2, 2 * FOUR_WIDTH), BF16),
        compiler_params=pltpu.CompilerParams(dimension_semantics=("arbitrary", "arbitrary")),
        name="fourier_a",
    )(x4, norm_in, w_in_pk, chan, stage_a, tw)


def _fourier_b_kernel(t_ref, stb_ref, f_ref, *, k1_per_step, scale):
    for k in range(k1_per_step):
        t = t_ref[k]
        tcat = jnp.concatenate([t[:, :FOUR_WIDTH], t[:, FOUR_WIDTH:]], axis=0)
        f_ref[k] = (_dot(stb_ref[...], tcat) * scale).astype(BF16)


def _fourier_b_call(t, stage_b, seq):
    b, n1, n2, _ = t.shape
    k1_per_step = max(1, FOURIER_B_STEP_TOKENS // n2)
    kern = functools.partial(_fourier_b_kernel, k1_per_step=k1_per_step,
                             scale=1.0 / math.sqrt(seq * FOUR_GDIM))
    return pl.pallas_call(
        kern,
        grid=(b, n1 // k1_per_step),
        in_specs=[pl.BlockSpec((None, k1_per_step, n2, 2 * FOUR_WIDTH), lambda i, j: (i, j, 0, 0)),
                  _const_spec((n2, 2 * n2))],
        out_specs=pl.BlockSpec((None, k1_per_step, n2, FOUR_WIDTH), lambda i, j: (i, j, 0, 0)),
        out_shape=jax.ShapeDtypeStruct((b, n1, n2, FOUR_WIDTH), BF16),
        compiler_params=pltpu.CompilerParams(dimension_semantics=("arbitrary", "arbitrary")),
        name="fourier_b",
    )(t, stage_b)


def _main_kernel(xm_ref, xp_ref, xq_ref, f_ref, kv_ref, gin_ref, win_ref, wpg_ref, psc_ref,
                 wpo_ref, wfo_ref, wao_ref, bg_ref, wo_ref, gf_ref, out_ref, uext_ref, *, seq, tile):
    i = pl.program_id(1)
    last = pl.num_programs(1) - 1
    gin = gin_ref[...]
    x = xm_ref[...]
    xn = _rmsnorm(x, gin).astype(BF16)

    def proj(off, width, lhs=xn):
        return _dot(lhs, _unpack_rows(win_ref[:, off:off + width]))

    def gate(branch):
        off = branch * D_MODEL
        return _sigmoid(proj(OFF_MG + off, D_MODEL) + bg_ref[:, off:off + D_MODEL])


    halo = jnp.concatenate([xp_ref[...], xq_ref[...]], axis=0)
    pv_ext = proj(OFF_PV, D_MODEL, jnp.concatenate([xn, _rmsnorm(halo, gin).astype(BF16)], axis=0))
    pv_prev = jnp.where(i > 0, pv_ext[tile:tile + POOL_HALO], 0.0)
    pv_next = jnp.where(i < last, pv_ext[tile + POOL_HALO:], 0.0)
    for s in range(D_MODEL // LANES):
        lanes = slice(s * LANES, (s + 1) * LANES)
        uext_ref[s, 0:POOL_HALO, :] = pv_prev[:, lanes]
        uext_ref[s, POOL_HALO:POOL_HALO + tile, :] = pv_ext[:tile, lanes]
        uext_ref[s, POOL_HALO + tile:, :] = pv_next[:, lanes]

    q = proj(OFF_Q, ATTN_WIDTH).astype(BF16)
    es, ls = [], []
    for h in range(ATTN_HEADS):
        kh = kv_ref[:, h * HEAD_DIM:(h + 1) * HEAD_DIM]
        s = lax.dot_general(q[:, h * HEAD_DIM:(h + 1) * HEAD_DIM], kh, (((1,), (1,)), ((), ())),
                            preferred_element_type=F32) * (1.0 / math.sqrt(HEAD_DIM))
        e = jnp.exp(s - jnp.max(s, axis=-1, keepdims=True))
        ls.append(jnp.sum(e, axis=-1, keepdims=True))
        es.append(e.astype(BF16))
    silu_pg = _silu(proj(OFF_PG, D_MODEL))

    t = i * tile + lax.broadcasted_iota(jnp.int32, (tile, 1), 0)
    slabs_per_group = POOL_GDIM // LANES
    ys = []
    for g, w in enumerate(POOL_WINDOWS):
        half = w // 2
        inv_cnt = 1.0 / (jnp.minimum(t + half, seq) - jnp.maximum(t - half, 0)).astype(F32)
        ps = []
        for s in range(g * slabs_per_group, (g + 1) * slabs_per_group):
            win = uext_ref[s, pl.ds(POOL_HALO - half, tile, stride=1), :]
            for jj in range(1, w):
                win = win + uext_ref[s, pl.ds(POOL_HALO - half + jj, tile, stride=1), :]
            ps.append(win * inv_cnt - uext_ref[s, POOL_HALO:POOL_HALO + tile, :])
        p = jnp.concatenate(ps, axis=1).astype(BF16)
        ys.append(_dot(p, _unpack_rows(wpg_ref[g])))
    ya_in = (jnp.concatenate(ys, axis=1) * psc_ref[...] * silu_pg).astype(BF16)

    silu_ag = _silu(proj(OFF_AG, ATTN_WIDTH))
    o = jnp.concatenate(
        [_dot(es[h], kv_ref[:, ATTN_WIDTH + h * HEAD_DIM:ATTN_WIDTH + (h + 1) * HEAD_DIM]) / ls[h]
         for h in range(ATTN_HEADS)], axis=1)
    yc_in = (o * silu_ag).astype(BF16)

    yb_in = (f_ref[...] * _silu(proj(OFF_FG, FOUR_WIDTH))).astype(BF16)

    merged = gate(0) * _dot(ya_in, _unpack_rows(wpo_ref[...]))
    merged = merged + gate(1) * _dot(yb_in, _unpack_rows(wfo_ref[...]))
    merged = merged + gate(2) * _dot(yc_in, _unpack_rows(wao_ref[...]))

    hres = x + _dot(merged.astype(BF16), _unpack_rows(wo_ref[...]))
    out_ref[...] = _rmsnorm(hres, gf_ref[...])


def _main_call(x, f, kv, norm_in, w_in_pk, w_pool_grp_pk, pool_scale,
               w_pool_out_pk, w_four_out_pk, w_attn_out_pk, b_gate, w_o_pk, norm_f):
    b, seq, _ = x.shape
    tile = TOKEN_TILE
    nt = seq // tile
    hb = tile // POOL_HALO
    nhb = seq // POOL_HALO
    kern = functools.partial(_main_kernel, seq=seq, tile=tile)
    return pl.pallas_call(
        kern,
        grid=(b, nt),
        in_specs=[pl.BlockSpec((None, tile, D_MODEL), lambda bi, i: (bi, i, 0)),
                  pl.BlockSpec((None, POOL_HALO, D_MODEL), lambda bi, i: (bi, jnp.maximum(i * hb - 1, 0), 0)),
                  pl.BlockSpec((None, POOL_HALO, D_MODEL),
                               lambda bi, i: (bi, jnp.minimum((i + 1) * hb, nhb - 1), 0)),
                  pl.BlockSpec((None, tile, FOUR_WIDTH), lambda bi, i: (bi, i, 0)),
                  pl.BlockSpec((None, N_MEM, 2 * ATTN_WIDTH), lambda bi, i: (bi, 0, 0)),
                  _const_spec((1, D_MODEL)),
                  _const_spec(w_in_pk.shape),
                  _const_spec(w_pool_grp_pk.shape),
                  _const_spec((1, D_MODEL)),
                  _const_spec(w_pool_out_pk.shape),
                  _const_spec(w_four_out_pk.shape),
                  _const_spec(w_attn_out_pk.shape),
                  _const_spec((1, 3 * D_MODEL)),
                  _const_spec(w_o_pk.shape),
                  _const_spec((1, D_MODEL))],
        out_specs=pl.BlockSpec((None, tile, D_MODEL), lambda bi, i: (bi, i, 0)),
        out_shape=jax.ShapeDtypeStruct((b, seq, D_MODEL), F32),
        scratch_shapes=[pltpu.VMEM((D_MODEL // LANES, tile + 2 * POOL_HALO, LANES), F32)],
        compiler_params=pltpu.CompilerParams(dimension_semantics=("arbitrary", "arbitrary"),
                                             vmem_limit_bytes=VMEM_LIMIT),
        name="encoder_main",
    )(x, x, x, f, kv, norm_in, w_in_pk, w_pool_grp_pk, pool_scale,
      w_pool_out_pk, w_four_out_pk, w_attn_out_pk, b_gate, w_o_pk, norm_f)


def _trunk(x, mem, norm_in, norm_mem, w_in_pk, w_pool_grp_pk, pool_scale, chan_pk, w_kv_pk,
           w_pool_out_pk, w_four_out_pk, w_attn_out_pk, b_gate, w_o_pk, norm_f):
    b, seq, _ = x.shape
    stage_a, tw, stage_b = _dft_constants(seq)
    stage_a, stage_b = (jnp.asarray(a).astype(BF16) for a in (stage_a, stage_b))
    kv = _kv_call(mem, norm_mem, w_kv_pk)
    t = _fourier_a_call(x, norm_in, w_in_pk, chan_pk, stage_a, tw)
    f = _fourier_b_call(t, stage_b, seq)
    f = jnp.swapaxes(f, 1, 2).reshape(b, seq, FOUR_WIDTH)
    return _main_call(x, f, kv, norm_in, w_in_pk, w_pool_grp_pk, pool_scale,
                      w_pool_out_pk, w_four_out_pk, w_attn_out_pk, b_gate, w_o_pk, norm_f)


def kernel(x_prompt, x_sample, mem_prompt, mem_sample, norm_in, norm_mem, w_in, w_pool_grp, pool_scale,
           w_four_grp, w_kv, w_pool_out, w_four_out, w_attn_out, b_gate, w_o, norm_f):
    assert norm_in.shape[0] == 1, "single-layer trunk"
    w_in_pk, w_pool_grp_pk, w_kv_pk, w_pool_out_pk, w_four_out_pk, w_attn_out_pk, w_o_pk = _pack_rows(
        w_in[0], w_pool_grp[0], w_kv[0], w_pool_out[0], w_four_out[0], w_attn_out[0], w_o[0])
    shared = (norm_in, norm_mem, w_in_pk, w_pool_grp_pk, pool_scale, _fold_chan_call(w_four_grp[0]), w_kv_pk,
              w_pool_out_pk, w_four_out_pk, w_attn_out_pk, b_gate, w_o_pk, norm_f.reshape(1, D_MODEL))
    return (_trunk(x_prompt, mem_prompt, *shared), _trunk(x_sample, mem_sample, *shared))
```

```python
import functools
import math

import jax
import jax.numpy as jnp
import numpy as np
from jax import lax
from jax.experimental import pallas as pl
from jax.experimental.pallas import tpu as pltpu

D_MODEL = 1024
EPS = 1e-6
POOL_WINDOWS = (2, 4, 8, 16)
POOL_GDIM = 256
POOL_HALO = 8
FOUR_GROUPS = 4
FOUR_GDIM = 128
FOUR_WIDTH = FOUR_GROUPS * FOUR_GDIM
ATTN_HEADS = 4
HEAD_DIM = 128
ATTN_WIDTH = ATTN_HEADS * HEAD_DIM
N_MEM = 256
OFF_PV, OFF_PG, OFF_FV, OFF_FG, OFF_Q, OFF_AG, OFF_MG = 0, 1024, 2048, 2560, 3072, 3584, 4096

SUBLANES = 8
LANES = 128
KRON_DEPTH = 256
FOURIER_A_STEP_TOKENS = 2048
FOURIER_B_STEP_TOKENS = 2048
TOKEN_TILE = 512
PACK_STEPS = 8
KV_BATCHES_PER_STEP = 4
VMEM_LIMIT = 56 * 1024 * 1024

BF16 = jnp.bfloat16
F32 = jnp.float32


def _dot(a, b):
    return jnp.dot(a, b, preferred_element_type=F32)


def _pack_kernel(*refs):
    n = len(refs) // 2
    for w_ref, o_ref in zip(refs[:n], refs[n:]):
        o_ref[...] = pltpu.bitcast(w_ref[...].astype(BF16), jnp.uint32)


def _pack_rows(*ws):
    flat = [w.reshape(-1, w.shape[-1]) for w in ws]
    in_specs, out_specs, out_shape = [], [], []
    for w in flat:
        k, n = w.shape
        in_specs.append(pl.BlockSpec((k // PACK_STEPS, n), lambda i: (i, 0)))
        out_specs.append(pl.BlockSpec((k // (2 * PACK_STEPS), n), lambda i: (i, 0)))
        out_shape.append(jax.ShapeDtypeStruct((k // 2, n), jnp.uint32))
    packed = pl.pallas_call(
        _pack_kernel,
        grid=(PACK_STEPS,),
        in_specs=in_specs,
        out_specs=out_specs,
        out_shape=out_shape,
        compiler_params=pltpu.CompilerParams(dimension_semantics=("arbitrary",)),
        name="pack_weights",
    )(*flat)
    return [p.reshape(*w.shape[:-2], w.shape[-2] // 2, w.shape[-1]) for p, w in zip(packed, ws)]


def _unpack_rows(w_u32):
    return pltpu.bitcast(w_u32, BF16)


def _rmsnorm(x, g):
    r = lax.rsqrt(jnp.mean(x * x, axis=-1, keepdims=True) + EPS)
    return x * r * g


def _sigmoid(x):
    return 0.5 * jnp.tanh(0.5 * x) + 0.5


def _silu(x):
    return x * _sigmoid(x)


def _const_spec(shape):
    nd = len(shape)
    return pl.BlockSpec(shape, lambda *_: (0,) * nd, pipeline_mode=pl.Buffered(1))


@functools.cache
def _channel_dft():
    c = np.arange(FOUR_GDIM)
    ang = 2.0 * np.pi * np.outer(c, c) / FOUR_GDIM
    return np.concatenate([np.cos(ang), -np.sin(ang)], axis=0).astype(np.float32)


def _dft_n1(seq):
    return min(KRON_DEPTH // SUBLANES, seq // (KRON_DEPTH // 2))


@functools.cache
def _dft_constants(seq):
    n1 = _dft_n1(seq)
    n2 = seq // n1
    group = KRON_DEPTH // n1
    k1 = np.arange(n1)
    psi = 2.0 * np.pi * np.outer(k1, k1) / n1
    eye = np.eye(group)
    mr = np.kron(np.cos(psi), eye)
    mi = np.kron(-np.sin(psi), eye)
    stage_a = np.stack([np.concatenate([mr, -mi], axis=1),
                        np.concatenate([mi, mr], axis=1)], axis=0)
    j = np.arange(n2 // group)
    r = np.arange(group)
    s2 = (group * j[:, None, None] + r[None, None, :])
    phi = 2.0 * np.pi * k1[None, :, None] * s2 / seq
    phi = phi.reshape(len(j), KRON_DEPTH, 1)
    tw = np.concatenate([np.broadcast_to(np.cos(phi), phi.shape[:2] + (LANES,)),
                         np.broadcast_to(-np.sin(phi), phi.shape[:2] + (LANES,))], axis=2)
    k2 = np.arange(n2)
    th = 2.0 * np.pi * np.outer(k2, k2) / n2
    stage_b = np.concatenate([np.cos(th), np.sin(th)], axis=1)
    return tuple(np.asarray(a, np.float32) for a in (stage_a, tw, stage_b))


def _fold_chan_kernel(c_ref, wg_ref, o_ref):
    for h in range(FOUR_GROUPS):
        r = jnp.dot(c_ref[...], wg_ref[h], precision=lax.Precision.HIGHEST, preferred_element_type=F32)
        folded = jnp.concatenate([r[:FOUR_GDIM], r[FOUR_GDIM:]], axis=1)
        o_ref[h] = pltpu.bitcast(folded.astype(BF16), jnp.uint32)


def _fold_chan_call(w_four_grp):
    return pl.pallas_call(
        _fold_chan_kernel,
        out_shape=jax.ShapeDtypeStruct((FOUR_GROUPS, FOUR_GDIM // 2, 2 * FOUR_GDIM), jnp.uint32),
        name="fold_chan",
    )(jnp.asarray(_channel_dft()), w_four_grp)


def _kv_kernel(mem_ref, g_ref, wkv_ref, kv_ref):
    nb = mem_ref.shape[0]
    mn = _rmsnorm(mem_ref[...].reshape(nb * N_MEM, D_MODEL), g_ref[...]).astype(BF16)
    kv_ref[...] = _dot(mn, _unpack_rows(wkv_ref[...])).astype(BF16).reshape(nb, N_MEM, 2 * ATTN_WIDTH)


def _kv_call(mem, norm_mem, w_kv_pk):
    b = mem.shape[0]
    per_step = math.gcd(b, KV_BATCHES_PER_STEP)
    return pl.pallas_call(
        _kv_kernel,
        grid=(b // per_step,),
        in_specs=[pl.BlockSpec((per_step, N_MEM, D_MODEL), lambda i: (i, 0, 0)),
                  _const_spec((1, D_MODEL)),
                  _const_spec(w_kv_pk.shape)],
        out_specs=pl.BlockSpec((per_step, N_MEM, 2 * ATTN_WIDTH), lambda i: (i, 0, 0)),
        out_shape=jax.ShapeDtypeStruct((b, N_MEM, 2 * ATTN_WIDTH), BF16),
        compiler_params=pltpu.CompilerParams(dimension_semantics=("arbitrary",)),
        name="kv_proj",
    )(mem, norm_mem, w_kv_pk)


def _fourier_a_kernel(x_ref, g_ref, wfv_ref, chan_ref, sta_ref, tw_ref, t_ref, *, n1, row_groups):
    group = KRON_DEPTH // n1
    step_rows = row_groups * group
    x = x_ref[...].reshape(n1 * step_rows, D_MODEL)
    xn = _rmsnorm(x, g_ref[...]).astype(BF16)
    fv = _dot(xn, _unpack_rows(wfv_ref[...])).astype(BF16)
    zr, zi = [], []
    for h in range(FOUR_GROUPS):
        z = _dot(fv[:, h * FOUR_GDIM:(h + 1) * FOUR_GDIM], _unpack_rows(chan_ref[h]))
        zr.append(z[:, :FOUR_GDIM])
        zi.append(z[:, FOUR_GDIM:])
    zr = jnp.concatenate(zr, axis=1).reshape(n1, step_rows, FOUR_WIDTH)
    zi = jnp.concatenate(zi, axis=1).reshape(n1, step_rows, FOUR_WIDTH)
    outs = []
    for g in range(row_groups):
        rs = slice(g * group, (g + 1) * group)
        zcat = jnp.concatenate([zr[:, rs, :].reshape(KRON_DEPTH, FOUR_WIDTH),
                                zi[:, rs, :].reshape(KRON_DEPTH, FOUR_WIDTH)], axis=0).astype(BF16)
        tr = _dot(sta_ref[0], zcat)
        ti = _dot(sta_ref[1], zcat)
        twr = tw_ref[g, :, :FOUR_GDIM]
        twi = tw_ref[g, :, FOUR_GDIM:]
        re, im = [], []
        for h in range(FOUR_GROUPS):
            a, b = tr[:, h * FOUR_GDIM:(h + 1) * FOUR_GDIM], ti[:, h * FOUR_GDIM:(h + 1) * FOUR_GDIM]
            re.append(a * twr - b * twi)
            im.append(a * twi + b * twr)
        outs.append(jnp.concatenate(re + im, axis=1).reshape(n1, group, 2 * FOUR_WIDTH))
    t_ref[...] = jnp.concatenate(outs, axis=1).astype(BF16)


def _fourier_a_call(x, norm_in, w_in_pk, chan, stage_a, tw):
    b, seq, _ = x.shape
    n1 = _dft_n1(seq)
    n2 = seq // n1
    step_rows = FOURIER_A_STEP_TOKENS // n1
    row_groups = step_rows // (KRON_DEPTH // n1)
    x4 = x.reshape(b, n1, n2, D_MODEL)
    return pl.pallas_call(
        functools.partial(_fourier_a_kernel, n1=n1, row_groups=row_groups),
        grid=(b, n2 // step_rows),
        in_specs=[pl.BlockSpec((None, n1, step_rows, D_MODEL), lambda i, j: (i, 0, j, 0)),
                  _const_spec((1, D_MODEL)),
                  pl.BlockSpec((D_MODEL // 2, FOUR_WIDTH), lambda i, j: (0, OFF_FV // FOUR_WIDTH),
                               pipeline_mode=pl.Buffered(1)),
                  _const_spec(chan.shape),
                  _const_spec((2, KRON_DEPTH, 2 * KRON_DEPTH)),
                  pl.BlockSpec((row_groups, KRON_DEPTH, 2 * FOUR_GDIM), lambda i, j: (j, 0, 0))],
        out_specs=pl.BlockSpec((None, n1, step_rows, 2 * FOUR_WIDTH), lambda i, j: (i, 0, j, 0)),
        out_shape=jax.ShapeDtypeStruct((b, n1, n2, 2 * FOUR_WIDTH), BF16),
        compiler_params=pltpu.CompilerParams(dimension_semantics=("arbitrary", "arbitrary")),
        name="fourier_a",
    )(x4, norm_in, w_in_pk, chan, stage_a, tw)


def _fourier_b_kernel(t_ref, stb_ref, f_ref, *, k1_per_step, scale):
    for k in range(k1_per_step):
        t = t_ref[k]
        tcat = jnp.concatenate([t[:, :FOUR_WIDTH], t[:, FOUR_WIDTH:]], axis=0)
        f_ref[k] = (_dot(stb_ref[...], tcat) * scale).astype(BF16)


def _fourier_b_call(t, stage_b, seq):
    b, n1, n2, _ = t.shape
    k1_per_step = max(1, FOURIER_B_STEP_TOKENS // n2)
    kern = functools.partial(_fourier_b_kernel, k1_per_step=k1_per_step,
                             scale=1.0 / math.sqrt(seq * FOUR_GDIM))
    return pl.pallas_call(
        kern,
        grid=(b, n1 // k1_per_step),
        in_specs=[pl.BlockSpec((None, k1_per_step, n2, 2 * FOUR_WIDTH), lambda i, j: (i, j, 0, 0)),
                  _const_spec((n2, 2 * n2))],
        out_specs=pl.BlockSpec((None, k1_per_step, n2, FOUR_WIDTH), lambda i, j: (i, j, 0, 0)),
        out_shape=jax.ShapeDtypeStruct((b, n1, n2, FOUR_WIDTH), BF16),
        compiler_params=pltpu.CompilerParams(dimension_semantics=("arbitrary", "arbitrary")),
        name="fourier_b",
    )(t, stage_b)


def _main_kernel(xm_ref, xp_ref, xq_ref, f_ref, kv_ref, gin_ref, win_ref, wpg_ref, psc_ref,
                 wpo_ref, wfo_ref, wao_ref, bg_ref, wo_ref, gf_ref, out_ref, uext_ref, *, seq, tile):
    i = pl.program_id(1)
    last = pl.num_programs(1) - 1
    gin = gin_ref[...]
    x = xm_ref[...]
    xn = _rmsnorm(x, gin).astype(BF16)

    def proj(off, width, lhs=xn):
        return _dot(lhs, _unpack_rows(win_ref[:, off:off + width]))

    def gate(branch):
        off = branch * D_MODEL
        return _sigmoid(proj(OFF_MG + off, D_MODEL) + bg_ref[:, off:off + D_MODEL])


    halo = jnp.concatenate([xp_ref[...], xq_ref[...]], axis=0)
    pv_ext = proj(OFF_PV, D_MODEL, jnp.concatenate([xn, _rmsnorm(halo, gin).astype(BF16)], axis=0))
    pv_prev = jnp.where(i > 0, pv_ext[tile:tile + POOL_HALO], 0.0)
    pv_next = jnp.where(i < last, pv_ext[tile + POOL_HALO:], 0.0)
    for s in range(D_MODEL // LANES):
        lanes = slice(s * LANES, (s + 1) * LANES)
        uext_ref[s, 0:POOL_HALO, :] = pv_prev[:, lanes]
        uext_ref[s, POOL_HALO:POOL_HALO + tile, :] = pv_ext[:tile, lanes]
        uext_ref[s, POOL_HALO + tile:, :] = pv_next[:, lanes]

    q = proj(OFF_Q, ATTN_WIDTH).astype(BF16)
    es, ls = [], []
    for h in range(ATTN_HEADS):
        kh = kv_ref[:, h * HEAD_DIM:(h + 1) * HEAD_DIM]
        s = lax.dot_general(q[:, h * HEAD_DIM:(h + 1) * HEAD_DIM], kh, (((1,), (1,)), ((), ())),
                            preferred_element_type=F32) * (1.0 / math.sqrt(HEAD_DIM))
        e = jnp.exp(s - jnp.max(s, axis=-1, keepdims=True))
        ls.append(jnp.sum(e, axis=-1, keepdims=True))
        es.append(e.astype(BF16))

    yb_in = (f_ref[...] * _silu(proj(OFF_FG, FOUR_WIDTH))).astype(BF16)

    silu_pg = _silu(proj(OFF_PG, D_MODEL))

    t = i * tile + lax.broadcasted_iota(jnp.int32, (tile, 1), 0)
    slabs_per_group = POOL_GDIM // LANES
    ys = []
    for g, w in enumerate(POOL_WINDOWS):
        half = w // 2
        inv_cnt = 1.0 / (jnp.minimum(t + half, seq) - jnp.maximum(t - half, 0)).astype(F32)
        ps = []
        for s in range(g * slabs_per_group, (g + 1) * slabs_per_group):
            win = uext_ref[s, pl.ds(POOL_HALO - half, tile, stride=1), :]
            for jj in range(1, w):
                win = win + uext_ref[s, pl.ds(POOL_HALO - half + jj, tile, stride=1), :]
            ps.append(win * inv_cnt - uext_ref[s, POOL_HALO:POOL_HALO + tile, :])
        p = jnp.concatenate(ps, axis=1).astype(BF16)
        ys.append(_dot(p, _unpack_rows(wpg_ref[g])))
    ya_in = (jnp.concatenate(ys, axis=1) * psc_ref[...] * silu_pg).astype(BF16)

    silu_ag = _silu(proj(OFF_AG, ATTN_WIDTH))
    o = jnp.concatenate(
        [_dot(es[h], kv_ref[:, ATTN_WIDTH + h * HEAD_DIM:ATTN_WIDTH + (h + 1) * HEAD_DIM]) / ls[h]
         for h in range(ATTN_HEADS)], axis=1)
    yc_in = (o * silu_ag).astype(BF16)

    merged = gate(0) * _dot(ya_in, _unpack_rows(wpo_ref[...]))
    merged = merged + gate(1) * _dot(yb_in, _unpack_rows(wfo_ref[...]))
    merged = merged + gate(2) * _dot(yc_in, _unpack_rows(wao_ref[...]))

    hres = x + _dot(merged.astype(BF16), _unpack_rows(wo_ref[...]))
    out_ref[...] = _rmsnorm(hres, gf_ref[...])


def _main_call(x, f, kv, norm_in, w_in_pk, w_pool_grp_pk, pool_scale,
               w_pool_out_pk, w_four_out_pk, w_attn_out_pk, b_gate, w_o_pk, norm_f):
    b, seq, _ = x.shape
    tile = TOKEN_TILE
    nt = seq // tile
    hb = tile // POOL_HALO
    nhb = seq // POOL_HALO
    kern = functools.partial(_main_kernel, seq=seq, tile=tile)
    return pl.pallas_call(
        kern,
        grid=(b, nt),
        in_specs=[pl.BlockSpec((None, tile, D_MODEL), lambda bi, i: (bi, i, 0)),
                  pl.BlockSpec((None, POOL_HALO, D_MODEL), lambda bi, i: (bi, jnp.maximum(i * hb - 1, 0), 0)),
                  pl.BlockSpec((None, POOL_HALO, D_MODEL),
                               lambda bi, i: (bi, jnp.minimum((i + 1) * hb, nhb - 1), 0)),
                  pl.BlockSpec((None, tile, FOUR_WIDTH), lambda bi, i: (bi, i, 0)),
                  pl.BlockSpec((None, N_MEM, 2 * ATTN_WIDTH), lambda bi, i: (bi, 0, 0)),
                  _const_spec((1, D_MODEL)),
                  _const_spec(w_in_pk.shape),
                  _const_spec(w_pool_grp_pk.shape),
                  _const_spec((1, D_MODEL)),
                  _const_spec(w_pool_out_pk.shape),
                  _const_spec(w_four_out_pk.shape),
                  _const_spec(w_attn_out_pk.shape),
                  _const_spec((1, 3 * D_MODEL)),
                  _const_spec(w_o_pk.shape),
                  _const_spec((1, D_MODEL))],
        out_specs=pl.BlockSpec((None, tile, D_MODEL), lambda bi, i: (bi, i, 0)),
        out_shape=jax.ShapeDtypeStruct((b, seq, D_MODEL), F32),
        scratch_shapes=[pltpu.VMEM((D_MODEL // LANES, tile + 2 * POOL_HALO, LANES), F32)],
        compiler_params=pltpu.CompilerParams(dimension_semantics=("arbitrary", "arbitrary"),
                                             vmem_limit_bytes=VMEM_LIMIT),
        name="encoder_main",
    )(x, x, x, f, kv, norm_in, w_in_pk, w_pool_grp_pk, pool_scale,
      w_pool_out_pk, w_four_out_pk, w_attn_out_pk, b_gate, w_o_pk, norm_f)


def _trunk(x, mem, norm_in, norm_mem, w_in_pk, w_pool_grp_pk, pool_scale, chan_pk, w_kv_pk,
           w_pool_out_pk, w_four_out_pk, w_attn_out_pk, b_gate, w_o_pk, norm_f):
    b, seq, _ = x.shape
    stage_a, tw, stage_b = _dft_constants(seq)
    stage_a, stage_b = (jnp.asarray(a).astype(BF16) for a in (stage_a, stage_b))
    kv = _kv_call(mem, norm_mem, w_kv_pk)
    t = _fourier_a_call(x, norm_in, w_in_pk, chan_pk, stage_a, tw)
    f = _fourier_b_call(t, stage_b, seq)
    f = jnp.swapaxes(f, 1, 2).reshape(b, seq, FOUR_WIDTH)
    return _main_call(x, f, kv, norm_in, w_in_pk, w_pool_grp_pk, pool_scale,
                      w_pool_out_pk, w_four_out_pk, w_attn_out_pk, b_gate, w_o_pk, norm_f)


def kernel(x_prompt, x_sample, mem_prompt, mem_sample, norm_in, norm_mem, w_in, w_pool_grp, pool_scale,
           w_four_grp, w_kv, w_pool_out, w_four_out, w_attn_out, b_gate, w_o, norm_f):
    assert norm_in.shape[0] == 1, "single-layer trunk"
    w_in_pk, w_pool_grp_pk, w_kv_pk, w_pool_out_pk, w_four_out_pk, w_attn_out_pk, w_o_pk = _pack_rows(
        w_in[0], w_pool_grp[0], w_kv[0], w_pool_out[0], w_four_out[0], w_attn_out[0], w_o[0])
    shared = (norm_in, norm_mem, w_in_pk, w_pool_grp_pk, pool_scale, _fold_chan_call(w_four_grp[0]), w_kv_pk,
              w_pool_out_pk, w_four_out_pk, w_attn_out_pk, b_gate, w_o_pk, norm_f.reshape(1, D_MODEL))
    return (_trunk(x_prompt, mem_prompt, *shared), _trunk(x_sample, mem_sample, *shared))
```

```python
import functools
import math

import jax
import jax.numpy as jnp
import numpy as np
from jax import lax
from jax.experimental import pallas as pl
from jax.experimental.pallas import tpu as pltpu

D_MODEL = 1024
EPS = 1e-6
POOL_WINDOWS = (2, 4, 8, 16)
POOL_GDIM = 256
POOL_HALO = 8
FOUR_GROUPS = 4
FOUR_GDIM = 128
FOUR_WIDTH = FOUR_GROUPS * FOUR_GDIM
ATTN_HEADS = 4
HEAD_DIM = 128
ATTN_WIDTH = ATTN_HEADS * HEAD_DIM
N_MEM = 256
OFF_PV, OFF_PG, OFF_FV, OFF_FG, OFF_Q, OFF_AG, OFF_MG = 0, 1024, 2048, 2560, 3072, 3584, 4096

SUBLANES = 8
LANES = 128
KRON_DEPTH = 256
FOURIER_A_STEP_TOKENS = 2048
FOURIER_B_STEP_TOKENS = 2048
TOKEN_TILE = 512
PACK_STEPS = 8
KV_BATCHES_PER_STEP = 4
VMEM_LIMIT = 56 * 1024 * 1024

BF16 = jnp.bfloat16
F32 = jnp.float32


def _dot(a, b):
    return jnp.dot(a, b, preferred_element_type=F32)


def _pack_kernel(*refs):
    n = len(refs) // 2
    for w_ref, o_ref in zip(refs[:n], refs[n:]):
        o_ref[...] = pltpu.bitcast(w_ref[...].astype(BF16), jnp.uint32)


def _pack_rows(*ws):
    flat = [w.reshape(-1, w.shape[-1]) for w in ws]
    in_specs, out_specs, out_shape = [], [], []
    for w in flat:
        k, n = w.shape
        in_specs.append(pl.BlockSpec((k // PACK_STEPS, n), lambda i: (i, 0)))
        out_specs.append(pl.BlockSpec((k // (2 * PACK_STEPS), n), lambda i: (i, 0)))
        out_shape.append(jax.ShapeDtypeStruct((k // 2, n), jnp.uint32))
    packed = pl.pallas_call(
        _pack_kernel,
        grid=(PACK_STEPS,),
        in_specs=in_specs,
        out_specs=out_specs,
        out_shape=out_shape,
        compiler_params=pltpu.CompilerParams(dimension_semantics=("arbitrary",)),
        name="pack_weights",
    )(*flat)
    return [p.reshape(*w.shape[:-2], w.shape[-2] // 2, w.shape[-1]) for p, w in zip(packed, ws)]


def _unpack_rows(w_u32):
    return pltpu.bitcast(w_u32, BF16)


def _rmsnorm(x, g):
    r = lax.rsqrt(jnp.mean(x * x, axis=-1, keepdims=True) + EPS)
    return x * r * g


def _sigmoid(x):
    return 0.5 * jnp.tanh(0.5 * x) + 0.5


def _silu(x):
    return x * _sigmoid(x)


def _const_spec(shape):
    nd = len(shape)
    return pl.BlockSpec(shape, lambda *_: (0,) * nd, pipeline_mode=pl.Buffered(1))


@functools.cache
def _channel_dft():
    c = np.arange(FOUR_GDIM)
    ang = 2.0 * np.pi * np.outer(c, c) / FOUR_GDIM
    return np.concatenate([np.cos(ang), -np.sin(ang)], axis=0).astype(np.float32)


def _dft_n1(seq):
    return min(KRON_DEPTH // SUBLANES, seq // (KRON_DEPTH // 2))


@functools.cache
def _dft_constants(seq):
    n1 = _dft_n1(seq)
    n2 = seq // n1
    group = KRON_DEPTH // n1
    k1 = np.arange(n1)
    psi = 2.0 * np.pi * np.outer(k1, k1) / n1
    eye = np.eye(group)
    mr = np.kron(np.cos(psi), eye)
    mi = np.kron(-np.sin(psi), eye)
    stage_a = np.stack([np.concatenate([mr, -mi], axis=1),
                        np.concatenate([mi, mr], axis=1)], axis=0)
    j = np.arange(n2 // group)
    r = np.arange(group)
    s2 = (group * j[:, None, None] + r[None, None, :])
    phi = 2.0 * np.pi * k1[None, :, None] * s2 / seq
    phi = phi.reshape(len(j), KRON_DEPTH, 1)
    tw = np.concatenate([np.broadcast_to(np.cos(phi), phi.shape[:2] + (LANES,)),
                         np.broadcast_to(-np.sin(phi), phi.shape[:2] + (LANES,))], axis=2)
    k2 = np.arange(n2)
    th = 2.0 * np.pi * np.outer(k2, k2) / n2
    stage_b = np.concatenate([np.cos(th), np.sin(th)], axis=1)
    return tuple(np.asarray(a, np.float32) for a in (stage_a, tw, stage_b))


def _fold_chan_kernel(c_ref, wg_ref, o_ref):
    for h in range(FOUR_GROUPS):
        r = jnp.dot(c_ref[...], wg_ref[h], precision=lax.Precision.HIGHEST, preferred_element_type=F32)
        folded = jnp.concatenate([r[:FOUR_GDIM], r[FOUR_GDIM:]], axis=1)
        o_ref[h] = pltpu.bitcast(folded.astype(BF16), jnp.uint32)


def _fold_chan_call(w_four_grp):
    return pl.pallas_call(
        _fold_chan_kernel,
        out_shape=jax.ShapeDtypeStruct((FOUR_GROUPS, FOUR_GDIM // 2, 2 * FOUR_GDIM), jnp.uint32),
        name="fold_chan",
    )(jnp.asarray(_channel_dft()), w_four_grp)


def _kv_kernel(mem_ref, g_ref, wkv_ref, kv_ref):
    nb = mem_ref.shape[0]
    mn = _rmsnorm(mem_ref[...].reshape(nb * N_MEM, D_MODEL), g_ref[...]).astype(BF16)
    kv_ref[...] = _dot(mn, _unpack_rows(wkv_ref[...])).astype(BF16).reshape(nb, N_MEM, 2 * ATTN_WIDTH)


def _kv_call(mem, norm_mem, w_kv_pk):
    b = mem.shape[0]
    per_step = math.gcd(b, KV_BATCHES_PER_STEP)
    return pl.pallas_call(
        _kv_kernel,
        grid=(b // per_step,),
        in_specs=[pl.BlockSpec((per_step, N_MEM, D_MODEL), lambda i: (i, 0, 0)),
                  _const_spec((1, D_MODEL)),
                  _const_spec(w_kv_pk.shape)],
        out_specs=pl.BlockSpec((per_step, N_MEM, 2 * ATTN_WIDTH), lambda i: (i, 0, 0)),
        out_shape=jax.ShapeDtypeStruct((b, N_MEM, 2 * ATTN_WIDTH), BF16),
        compiler_params=pltpu.CompilerParams(dimension_semantics=("arbitrary",)),
        name="kv_proj",
    )(mem, norm_mem, w_kv_pk)


def _stage_a(x_ref, g_ref, wfv_ref, chan_ref, sta_ref, tw_ref, n1, row_groups):
    group = KRON_DEPTH // n1
    step_rows = row_groups * group
    x = x_ref[...].reshape(n1 * step_rows, D_MODEL)
    xn = _rmsnorm(x, g_ref[...]).astype(BF16)
    fv = _dot(xn, _unpack_rows(wfv_ref[...])).astype(BF16)
    zr, zi = [], []
    for h in range(FOUR_GROUPS):
        z = _dot(fv[:, h * FOUR_GDIM:(h + 1) * FOUR_GDIM], _unpack_rows(chan_ref[h]))
        zr.append(z[:, :FOUR_GDIM])
        zi.append(z[:, FOUR_GDIM:])
    zr = jnp.concatenate(zr, axis=1).reshape(n1, step_rows, FOUR_WIDTH)
    zi = jnp.concatenate(zi, axis=1).reshape(n1, step_rows, FOUR_WIDTH)
    outs = []
    for g in range(row_groups):
        rs = slice(g * group, (g + 1) * group)
        zcat = jnp.concatenate([zr[:, rs, :].reshape(KRON_DEPTH, FOUR_WIDTH),
                                zi[:, rs, :].reshape(KRON_DEPTH, FOUR_WIDTH)], axis=0).astype(BF16)
        tr = _dot(sta_ref[0], zcat)
        ti = _dot(sta_ref[1], zcat)
        twr = tw_ref[g, :, :FOUR_GDIM]
        twi = tw_ref[g, :, FOUR_GDIM:]
        re, im = [], []
        for h in range(FOUR_GROUPS):
            a, b = tr[:, h * FOUR_GDIM:(h + 1) * FOUR_GDIM], ti[:, h * FOUR_GDIM:(h + 1) * FOUR_GDIM]
            re.append(a * twr - b * twi)
            im.append(a * twi + b * twr)
        outs.append(jnp.concatenate(re + im, axis=1).reshape(n1, group, 2 * FOUR_WIDTH))
    return jnp.concatenate(outs, axis=1).astype(BF16)


def _stage_b(t, stb, scale):
    tcat = jnp.concatenate([t[:, :FOUR_WIDTH], t[:, FOUR_WIDTH:]], axis=0)
    return (_dot(stb, tcat) * scale).astype(BF16)


def _fourier_a_kernel(x_ref, g_ref, wfv_ref, chan_ref, sta_ref, tw_ref, t_ref, *, n1, row_groups):
    t_ref[...] = _stage_a(x_ref, g_ref, wfv_ref, chan_ref, sta_ref, tw_ref, n1, row_groups)


def _fourier_a_call(x, norm_in, w_in_pk, chan, stage_a, tw):
    b, seq, _ = x.shape
    n1 = _dft_n1(seq)
    n2 = seq // n1
    step_rows = FOURIER_A_STEP_TOKENS // n1
    row_groups = step_rows // (KRON_DEPTH // n1)
    x4 = x.reshape(b, n1, n2, D_MODEL)
    return pl.pallas_call(
        functools.partial(_fourier_a_kernel, n1=n1, row_groups=row_groups),
        grid=(b, n2 // step_rows),
        in_specs=[pl.BlockSpec((None, n1, step_rows, D_MODEL), lambda i, j: (i, 0, j, 0)),
                  _const_spec((1, D_MODEL)),
                  pl.BlockSpec((D_MODEL // 2, FOUR_WIDTH), lambda i, j: (0, OFF_FV // FOUR_WIDTH),
                               pipeline_mode=pl.Buffered(1)),
                  _const_spec(chan.shape),
                  _const_spec((2, KRON_DEPTH, 2 * KRON_DEPTH)),
                  pl.BlockSpec((row_groups, KRON_DEPTH, 2 * FOUR_GDIM), lambda i, j: (j, 0, 0))],
        out_specs=pl.BlockSpec((None, n1, step_rows, 2 * FOUR_WIDTH), lambda i, j: (i, 0, j, 0)),
        out_shape=jax.ShapeDtypeStruct((b, n1, n2, 2 * FOUR_WIDTH), BF16),
        compiler_params=pltpu.CompilerParams(dimension_semantics=("arbitrary", "arbitrary")),
        name="fourier_a",
    )(x4, norm_in, w_in_pk, chan, stage_a, tw)


def _fourier_b_kernel(t_ref, stb_ref, f_ref, *, k1_per_step, scale):
    for k in range(k1_per_step):
        f_ref[k] = _stage_b(t_ref[k], stb_ref[...], scale)


def _fourier_b_call(t, stage_b, seq):
    b, n1, n2, _ = t.shape
    k1_per_step = max(1, FOURIER_B_STEP_TOKENS // n2)
    kern = functools.partial(_fourier_b_kernel, k1_per_step=k1_per_step,
                             scale=1.0 / math.sqrt(seq * FOUR_GDIM))
    return pl.pallas_call(
        kern,
        grid=(b, n1 // k1_per_step),
        in_specs=[pl.BlockSpec((None, k1_per_step, n2, 2 * FOUR_WIDTH), lambda i, j: (i, j, 0, 0)),
                  _const_spec((n2, 2 * n2))],
        out_specs=pl.BlockSpec((None, k1_per_step, n2, FOUR_WIDTH), lambda i, j: (i, j, 0, 0)),
        out_shape=jax.ShapeDtypeStruct((b, n1, n2, FOUR_WIDTH), BF16),
        compiler_params=pltpu.CompilerParams(dimension_semantics=("arbitrary", "arbitrary")),
        name="fourier_b",
    )(t, stage_b)


def _fourier_ab_kernel(x_ref, g_ref, wfv_ref, chan_ref, sta_ref, tw_ref, stb_ref, f_ref, t_ref, *, n1, row_groups, scale):
    t_ref[...] = _stage_a(x_ref, g_ref, wfv_ref, chan_ref, sta_ref, tw_ref, n1, row_groups)
    for k in range(n1):
        f_ref[k] = _stage_b(t_ref[k], stb_ref[...], scale)


def _fourier_ab_call(x, norm_in, w_in_pk, chan, stage_a, tw, stage_b):
    b, seq, _ = x.shape
    n1 = _dft_n1(seq)
    n2 = seq // n1
    row_groups = n2 // (KRON_DEPTH // n1)
    x4 = x.reshape(b, n1, n2, D_MODEL)
    kern = functools.partial(_fourier_ab_kernel, n1=n1, row_groups=row_groups, scale=1.0 / math.sqrt(seq * FOUR_GDIM))
    return pl.pallas_call(
        kern,
        grid=(b,),
        in_specs=[pl.BlockSpec((None, n1, n2, D_MODEL), lambda i: (i, 0, 0, 0)),
                  _const_spec((1, D_MODEL)),
                  pl.BlockSpec((D_MODEL // 2, FOUR_WIDTH), lambda i: (0, OFF_FV // FOUR_WIDTH),
                               pipeline_mode=pl.Buffered(1)),
                  _const_spec(chan.shape),
                  _const_spec((2, KRON_DEPTH, 2 * KRON_DEPTH)),
                  _const_spec(tw.shape),
                  _const_spec((n2, 2 * n2))],
        out_specs=pl.BlockSpec((None, n1, n2, FOUR_WIDTH), lambda i: (i, 0, 0, 0)),
        out_shape=jax.ShapeDtypeStruct((b, n1, n2, FOUR_WIDTH), BF16),
        scratch_shapes=[pltpu.VMEM((n1, n2, 2 * FOUR_WIDTH), BF16)],
        compiler_params=pltpu.CompilerParams(dimension_semantics=("arbitrary",), vmem_limit_bytes=VMEM_LIMIT),
        name="fourier_ab",
    )(x4, norm_in, w_in_pk, chan, stage_a, tw, stage_b)


def _main_kernel(xm_ref, xp_ref, xq_ref, f_ref, kv_ref, gin_ref, win_ref, wpg_ref, psc_ref,
                 wpo_ref, wfo_ref, wao_ref, bg_ref, wo_ref, gf_ref, out_ref, uext_ref, *, seq, tile):
    i = pl.program_id(1)
    last = pl.num_programs(1) - 1
    gin = gin_ref[...]
    x = xm_ref[...]
    xn = _rmsnorm(x, gin).astype(BF16)

    def proj(off, width, lhs=xn):
        return _dot(lhs, _unpack_rows(win_ref[:, off:off + width]))

    def gate(branch):
        off = branch * D_MODEL
        return _sigmoid(proj(OFF_MG + off, D_MODEL) + bg_ref[:, off:off + D_MODEL])


    halo = jnp.concatenate([xp_ref[...], xq_ref[...]], axis=0)
    pv_ext = proj(OFF_PV, D_MODEL, jnp.concatenate([xn, _rmsnorm(halo, gin).astype(BF16)], axis=0))
    pv_prev = jnp.where(i > 0, pv_ext[tile:tile + POOL_HALO], 0.0)
    pv_next = jnp.where(i < last, pv_ext[tile + POOL_HALO:], 0.0)
    for s in range(D_MODEL // LANES):
        lanes = slice(s * LANES, (s + 1) * LANES)
        uext_ref[s, 0:POOL_HALO, :] = pv_prev[:, lanes]
        uext_ref[s, POOL_HALO:POOL_HALO + tile, :] = pv_ext[:tile, lanes]
        uext_ref[s, POOL_HALO + tile:, :] = pv_next[:, lanes]

    q = proj(OFF_Q, ATTN_WIDTH).astype(BF16)
    es, ls = [], []
    for h in range(ATTN_HEADS):
        kh = kv_ref[:, h * HEAD_DIM:(h + 1) * HEAD_DIM]
        s = lax.dot_general(q[:, h * HEAD_DIM:(h + 1) * HEAD_DIM], kh, (((1,), (1,)), ((), ())),
                            preferred_element_type=F32) * (1.0 / math.sqrt(HEAD_DIM))
        e = jnp.exp(s - jnp.max(s, axis=-1, keepdims=True))
        ls.append(jnp.sum(e, axis=-1, keepdims=True))
        es.append(e.astype(BF16))

    yb_in = (f_ref[...] * _silu(proj(OFF_FG, FOUR_WIDTH))).astype(BF16)

    silu_pg = _silu(proj(OFF_PG, D_MODEL))

    t = i * tile + lax.broadcasted_iota(jnp.int32, (tile, 1), 0)
    slabs_per_group = POOL_GDIM // LANES
    ys = []
    for g, w in enumerate(POOL_WINDOWS):
        half = w // 2
        inv_cnt = 1.0 / (jnp.minimum(t + half, seq) - jnp.maximum(t - half, 0)).astype(F32)
        ps = []
        for s in range(g * slabs_per_group, (g + 1) * slabs_per_group):
            win = uext_ref[s, pl.ds(POOL_HALO - half, tile, stride=1), :]
            for jj in range(1, w):
                win = win + uext_ref[s, pl.ds(POOL_HALO - half + jj, tile, stride=1), :]
            ps.append(win * inv_cnt - uext_ref[s, POOL_HALO:POOL_HALO + tile, :])
        p = jnp.concatenate(ps, axis=1).astype(BF16)
        ys.append(_dot(p, _unpack_rows(wpg_ref[g])))
    ya_in = (jnp.concatenate(ys, axis=1) * psc_ref[...] * silu_pg).astype(BF16)

    silu_ag = _silu(proj(OFF_AG, ATTN_WIDTH))
    o = jnp.concatenate(
        [_dot(es[h], kv_ref[:, ATTN_WIDTH + h * HEAD_DIM:ATTN_WIDTH + (h + 1) * HEAD_DIM]) / ls[h]
         for h in range(ATTN_HEADS)], axis=1)
    yc_in = (o * silu_ag).astype(BF16)

    merged = gate(0) * _dot(ya_in, _unpack_rows(wpo_ref[...]))
    merged = merged + gate(1) * _dot(yb_in, _unpack_rows(wfo_ref[...]))
    merged = merged + gate(2) * _dot(yc_in, _unpack_rows(wao_ref[...]))

    hres = x + _dot(merged.astype(BF16), _unpack_rows(wo_ref[...]))
    out_ref[...] = _rmsnorm(hres, gf_ref[...])


def _main_call(x, f, kv, norm_in, w_in_pk, w_pool_grp_pk, pool_scale,
               w_pool_out_pk, w_four_out_pk, w_attn_out_pk, b_gate, w_o_pk, norm_f):
    b, seq, _ = x.shape
    tile = TOKEN_TILE
    nt = seq // tile
    hb = tile // POOL_HALO
    nhb = seq // POOL_HALO
    kern = functools.partial(_main_kernel, seq=seq, tile=tile)
    return pl.pallas_call(
        kern,
        grid=(b, nt),
        in_specs=[pl.BlockSpec((None, tile, D_MODEL), lambda bi, i: (bi, i, 0)),
                  pl.BlockSpec((None, POOL_HALO, D_MODEL), lambda bi, i: (bi, jnp.maximum(i * hb - 1, 0), 0)),
                  pl.BlockSpec((None, POOL_HALO, D_MODEL),
                               lambda bi, i: (bi, jnp.minimum((i + 1) * hb, nhb - 1), 0)),
                  pl.BlockSpec((None, tile, FOUR_WIDTH), lambda bi, i: (bi, i, 0)),
                  pl.BlockSpec((None, N_MEM, 2 * ATTN_WIDTH), lambda bi, i: (bi, 0, 0)),
                  _const_spec((1, D_MODEL)),
                  _const_spec(w_in_pk.shape),
                  _const_spec(w_pool_grp_pk.shape),
                  _const_spec((1, D_MODEL)),
                  _const_spec(w_pool_out_pk.shape),
                  _const_spec(w_four_out_pk.shape),
                  _const_spec(w_attn_out_pk.shape),
                  _const_spec((1, 3 * D_MODEL)),
                  _const_spec(w_o_pk.shape),
                  _const_spec((1, D_MODEL))],
        out_specs=pl.BlockSpec((None, tile, D_MODEL), lambda bi, i: (bi, i, 0)),
        out_shape=jax.ShapeDtypeStruct((b, seq, D_MODEL), F32),
        scratch_shapes=[pltpu.VMEM((D_MODEL // LANES, tile + 2 * POOL_HALO, LANES), F32)],
        compiler_params=pltpu.CompilerParams(dimension_semantics=("arbitrary", "arbitrary"),
                                             vmem_limit_bytes=VMEM_LIMIT),
        name="encoder_main",
    )(x, x, x, f, kv, norm_in, w_in_pk, w_pool_grp_pk, pool_scale,
      w_pool_out_pk, w_four_out_pk, w_attn_out_pk, b_gate, w_o_pk, norm_f)


def _trunk(x, mem, norm_in, norm_mem, w_in_pk, w_pool_grp_pk, pool_scale, chan_pk, w_kv_pk,
           w_pool_out_pk, w_four_out_pk, w_attn_out_pk, b_gate, w_o_pk, norm_f):
    b, seq, _ = x.shape
    stage_a, tw, stage_b = _dft_constants(seq)
    stage_a, stage_b = (jnp.asarray(a).astype(BF16) for a in (stage_a, stage_b))
    kv = _kv_call(mem, norm_mem, w_kv_pk)
    if seq <= FOURIER_A_STEP_TOKENS:
        f = _fourier_ab_call(x, norm_in, w_in_pk, chan_pk, stage_a, tw, stage_b)
    else:
        t = _fourier_a_call(x, norm_in, w_in_pk, chan_pk, stage_a, tw)
        f = _fourier_b_call(t, stage_b, seq)
    f = jnp.swapaxes(f, 1, 2).reshape(b, seq, FOUR_WIDTH)
    return _main_call(x, f, kv, norm_in, w_in_pk, w_pool_grp_pk, pool_scale,
                      w_pool_out_pk, w_four_out_pk, w_attn_out_pk, b_gate, w_o_pk, norm_f)


def kernel(x_prompt, x_sample, mem_prompt, mem_sample, norm_in, norm_mem, w_in, w_pool_grp, pool_scale,
           w_four_grp, w_kv, w_pool_out, w_four_out, w_attn_out, b_gate, w_o, norm_f):
    assert norm_in.shape[0] == 1, "single-layer trunk"
    w_in_pk, w_pool_grp_pk, w_kv_pk, w_pool_out_pk, w_four_out_pk, w_attn_out_pk, w_o_pk = _pack_rows(
        w_in[0], w_pool_grp[0], w_kv[0], w_pool_out[0], w_four_out[0], w_attn_out[0], w_o[0])
    shared = (norm_in, norm_mem, w_in_pk, w_pool_grp_pk, pool_scale, _fold_chan_call(w_four_grp[0]), w_kv_pk,
              w_pool_out_pk, w_four_out_pk, w_attn_out_pk, b_gate, w_o_pk, norm_f.reshape(1, D_MODEL))
    return (_trunk(x_prompt, mem_prompt, *shared), _trunk(x_sample, mem_sample, *shared))
```

```python
import functools
import math

import jax
import jax.numpy as jnp
import numpy as np
from jax import lax
from jax.experimental import pallas as pl
from jax.experimental.pallas import tpu as pltpu

D_MODEL = 1024
EPS = 1e-6
POOL_WINDOWS = (2, 4, 8, 16)
POOL_GDIM = 256
POOL_HALO = 8
FOUR_GROUPS = 4
FOUR_GDIM = 128
FOUR_WIDTH = FOUR_GROUPS * FOUR_GDIM
ATTN_HEADS = 4
HEAD_DIM = 128
ATTN_WIDTH = ATTN_HEADS * HEAD_DIM
N_MEM = 256
OFF_PV, OFF_PG, OFF_FV, OFF_FG, OFF_Q, OFF_AG, OFF_MG = 0, 1024, 2048, 2560, 3072, 3584, 4096

SUBLANES = 8
LANES = 128
KRON_DEPTH = 256
FOURIER_A_STEP_TOKENS = 2048
FOURIER_B_STEP_TOKENS = 2048
TOKEN_TILE = 512
PACK_STEPS = 8
KV_BATCHES_PER_STEP = 4
VMEM_LIMIT = 56 * 1024 * 1024

BF16 = jnp.bfloat16
F32 = jnp.float32


def _dot(a, b):
    return jnp.dot(a, b, preferred_element_type=F32)


def _pack_kernel(*refs):
    n = len(refs) // 2
    for w_ref, o_ref in zip(refs[:n], refs[n:]):
        o_ref[...] = pltpu.bitcast(w_ref[...].astype(BF16), jnp.uint32)


def _pack_rows(*ws):
    flat = [w.reshape(-1, w.shape[-1]) for w in ws]
    in_specs, out_specs, out_shape = [], [], []
    for w in flat:
        k, n = w.shape
        in_specs.append(pl.BlockSpec((k // PACK_STEPS, n), lambda i: (i, 0)))
        out_specs.append(pl.BlockSpec((k // (2 * PACK_STEPS), n), lambda i: (i, 0)))
        out_shape.append(jax.ShapeDtypeStruct((k // 2, n), jnp.uint32))
    packed = pl.pallas_call(
        _pack_kernel,
        grid=(PACK_STEPS,),
        in_specs=in_specs,
        out_specs=out_specs,
        out_shape=out_shape,
        compiler_params=pltpu.CompilerParams(dimension_semantics=("arbitrary",)),
        name="pack_weights",
    )(*flat)
    return [p.reshape(*w.shape[:-2], w.shape[-2] // 2, w.shape[-1]) for p, w in zip(packed, ws)]


def _unpack_rows(w_u32):
    return pltpu.bitcast(w_u32, BF16)


def _rmsnorm(x, g):
    r = lax.rsqrt(jnp.mean(x * x, axis=-1, keepdims=True) + EPS)
    return x * r * g


def _sigmoid(x):
    return 0.5 * jnp.tanh(0.5 * x) + 0.5


def _silu(x):
    return x * _sigmoid(x)


def _const_spec(shape):
    nd = len(shape)
    return pl.BlockSpec(shape, lambda *_: (0,) * nd, pipeline_mode=pl.Buffered(1))


@functools.cache
def _channel_dft():
    c = np.arange(FOUR_GDIM)
    ang = 2.0 * np.pi * np.outer(c, c) / FOUR_GDIM
    return np.concatenate([np.cos(ang), -np.sin(ang)], axis=0).astype(np.float32)


def _dft_n1(seq):
    return min(KRON_DEPTH // SUBLANES, seq // (KRON_DEPTH // 2))


@functools.cache
def _dft_constants(seq):
    n1 = _dft_n1(seq)
    n2 = seq // n1
    group = KRON_DEPTH // n1
    k1 = np.arange(n1)
    psi = 2.0 * np.pi * np.outer(k1, k1) / n1
    eye = np.eye(group)
    mr = np.kron(np.cos(psi), eye)
    mi = np.kron(-np.sin(psi), eye)
    stage_a = np.stack([np.concatenate([mr, -mi], axis=1),
                        np.concatenate([mi, mr], axis=1)], axis=0)
    j = np.arange(n2 // group)
    r = np.arange(group)
    s2 = (group * j[:, None, None] + r[None, None, :])
    phi = 2.0 * np.pi * k1[None, :, None] * s2 / seq
    phi = phi.reshape(len(j), KRON_DEPTH, 1)
    tw = np.concatenate([np.broadcast_to(np.cos(phi), phi.shape[:2] + (LANES,)),
                         np.broadcast_to(-np.sin(phi), phi.shape[:2] + (LANES,))], axis=2)
    k2 = np.arange(n2)
    th = 2.0 * np.pi * np.outer(k2, k2) / n2
    stage_b = np.concatenate([np.cos(th), np.sin(th)], axis=1)
    return tuple(np.asarray(a, np.float32) for a in (stage_a, tw, stage_b))


def _fold_chan_kernel(c_ref, wg_ref, o_ref):
    for h in range(FOUR_GROUPS):
        r = jnp.dot(c_ref[...], wg_ref[h], precision=lax.Precision.HIGHEST, preferred_element_type=F32)
        folded = jnp.concatenate([r[:FOUR_GDIM], r[FOUR_GDIM:]], axis=1)
        o_ref[h] = pltpu.bitcast(folded.astype(BF16), jnp.uint32)


def _fold_chan_call(w_four_grp):
    return pl.pallas_call(
        _fold_chan_kernel,
        out_shape=jax.ShapeDtypeStruct((FOUR_GROUPS, FOUR_GDIM // 2, 2 * FOUR_GDIM), jnp.uint32),
        name="fold_chan",
    )(jnp.asarray(_channel_dft()), w_four_grp)


def _kv_kernel(mem_ref, g_ref, wkv_ref, kv_ref):
    nb = mem_ref.shape[0]
    mn = _rmsnorm(mem_ref[...].reshape(nb * N_MEM, D_MODEL), g_ref[...]).astype(BF16)
    kv_ref[...] = _dot(mn, _unpack_rows(wkv_ref[...])).astype(BF16).reshape(nb, N_MEM, 2 * ATTN_WIDTH)


def _kv_call(mem, norm_mem, w_kv_pk):
    b = mem.shape[0]
    per_step = math.gcd(b, KV_BATCHES_PER_STEP)
    return pl.pallas_call(
        _kv_kernel,
        grid=(b // per_step,),
        in_specs=[pl.BlockSpec((per_step, N_MEM, D_MODEL), lambda i: (i, 0, 0)),
                  _const_spec((1, D_MODEL)),
                  _const_spec(w_kv_pk.shape)],
        out_specs=pl.BlockSpec((per_step, N_MEM, 2 * ATTN_WIDTH), lambda i: (i, 0, 0)),
        out_shape=jax.ShapeDtypeStruct((b, N_MEM, 2 * ATTN_WIDTH), BF16),
        compiler_params=pltpu.CompilerParams(dimension_semantics=("arbitrary",)),
        name="kv_proj",
    )(mem, norm_mem, w_kv_pk)


def _stage_a(x_ref, g_ref, wfv_ref, chan_ref, sta_ref, tw_ref, n1, row_groups):
    group = KRON_DEPTH // n1
    step_rows = row_groups * group
    x = x_ref[...].reshape(n1 * step_rows, D_MODEL)
    xn = _rmsnorm(x, g_ref[...]).astype(BF16)
    fv = _dot(xn, _unpack_rows(wfv_ref[...])).astype(BF16)
    zr, zi = [], []
    for h in range(FOUR_GROUPS):
        z = _dot(fv[:, h * FOUR_GDIM:(h + 1) * FOUR_GDIM], _unpack_rows(chan_ref[h]))
        zr.append(z[:, :FOUR_GDIM])
        zi.append(z[:, FOUR_GDIM:])
    zr = jnp.concatenate(zr, axis=1).reshape(n1, step_rows, FOUR_WIDTH)
    zi = jnp.concatenate(zi, axis=1).reshape(n1, step_rows, FOUR_WIDTH)
    outs = []
    for g in range(row_groups):
        rs = slice(g * group, (g + 1) * group)
        zcat = jnp.concatenate([zr[:, rs, :].reshape(KRON_DEPTH, FOUR_WIDTH),
                                zi[:, rs, :].reshape(KRON_DEPTH, FOUR_WIDTH)], axis=0).astype(BF16)
        tr = _dot(sta_ref[0], zcat)
        ti = _dot(sta_ref[1], zcat)
        twr = tw_ref[g, :, :FOUR_GDIM]
        twi = tw_ref[g, :, FOUR_GDIM:]
        re, im = [], []
        for h in range(FOUR_GROUPS):
            a, b = tr[:, h * FOUR_GDIM:(h + 1) * FOUR_GDIM], ti[:, h * FOUR_GDIM:(h + 1) * FOUR_GDIM]
            re.append(a * twr - b * twi)
            im.append(a * twi + b * twr)
        outs.append(jnp.concatenate(re + im, axis=1).reshape(n1, group, 2 * FOUR_WIDTH))
    return jnp.concatenate(outs, axis=1).astype(BF16)


def _stage_b(t, stb, scale):
    tcat = jnp.concatenate([t[:, :FOUR_WIDTH], t[:, FOUR_WIDTH:]], axis=0)
    return (_dot(stb, tcat) * scale).astype(BF16)


def _fourier_a_kernel(x_ref, g_ref, wfv_ref, chan_ref, sta_ref, tw_ref, t_ref, *, n1, row_groups):
    t_ref[...] = _stage_a(x_ref, g_ref, wfv_ref, chan_ref, sta_ref, tw_ref, n1, row_groups)


def _fourier_a_call(x, norm_in, w_in_pk, chan, stage_a, tw):
    b, seq, _ = x.shape
    n1 = _dft_n1(seq)
    n2 = seq // n1
    step_rows = FOURIER_A_STEP_TOKENS // n1
    row_groups = step_rows // (KRON_DEPTH // n1)
    x4 = x.reshape(b, n1, n2, D_MODEL)
    return pl.pallas_call(
        functools.partial(_fourier_a_kernel, n1=n1, row_groups=row_groups),
        grid=(b, n2 // step_rows),
        in_specs=[pl.BlockSpec((None, n1, step_rows, D_MODEL), lambda i, j: (i, 0, j, 0)),
                  _const_spec((1, D_MODEL)),
                  pl.BlockSpec((D_MODEL // 2, FOUR_WIDTH), lambda i, j: (0, OFF_FV // FOUR_WIDTH),
                               pipeline_mode=pl.Buffered(1)),
                  _const_spec(chan.shape),
                  _const_spec((2, KRON_DEPTH, 2 * KRON_DEPTH)),
                  pl.BlockSpec((row_groups, KRON_DEPTH, 2 * FOUR_GDIM), lambda i, j: (j, 0, 0))],
        out_specs=pl.BlockSpec((None, n1, step_rows, 2 * FOUR_WIDTH), lambda i, j: (i, 0, j, 0)),
        out_shape=jax.ShapeDtypeStruct((b, n1, n2, 2 * FOUR_WIDTH), BF16),
        compiler_params=pltpu.CompilerParams(dimension_semantics=("arbitrary", "arbitrary")),
        name="fourier_a",
    )(x4, norm_in, w_in_pk, chan, stage_a, tw)


def _fourier_b_kernel(t_ref, stb_ref, f_ref, *, k1_per_step, scale):
    for k in range(k1_per_step):
        f_ref[k] = _stage_b(t_ref[k], stb_ref[...], scale)


def _fourier_b_call(t, stage_b, seq):
    b, n1, n2, _ = t.shape
    k1_per_step = max(1, FOURIER_B_STEP_TOKENS // n2)
    kern = functools.partial(_fourier_b_kernel, k1_per_step=k1_per_step,
                             scale=1.0 / math.sqrt(seq * FOUR_GDIM))
    return pl.pallas_call(
        kern,
        grid=(b, n1 // k1_per_step),
        in_specs=[pl.BlockSpec((None, k1_per_step, n2, 2 * FOUR_WIDTH), lambda i, j: (i, j, 0, 0)),
                  _const_spec((n2, 2 * n2))],
        out_specs=pl.BlockSpec((None, k1_per_step, n2, FOUR_WIDTH), lambda i, j: (i, j, 0, 0)),
        out_shape=jax.ShapeDtypeStruct((b, n1, n2, FOUR_WIDTH), BF16),
        compiler_params=pltpu.CompilerParams(dimension_semantics=("arbitrary", "arbitrary")),
        name="fourier_b",
    )(t, stage_b)


def _fourier_ab_kernel(x_ref, g_ref, wfv_ref, chan_ref, sta_ref, tw_ref, stb_ref, mem_ref, gmem_ref, wkv_ref,
                       f_ref, kv_ref, t_ref, *, n1, row_groups, scale):
    _kv_kernel(mem_ref, gmem_ref, wkv_ref, kv_ref)
    t_ref[...] = _stage_a(x_ref, g_ref, wfv_ref, chan_ref, sta_ref, tw_ref, n1, row_groups)
    for k in range(n1):
        f_ref[k] = _stage_b(t_ref[k], stb_ref[...], scale)


def _fourier_ab_call(x, norm_in, w_in_pk, chan, stage_a, tw, stage_b, mem, norm_mem, w_kv_pk):
    b, seq, _ = x.shape
    n1 = _dft_n1(seq)
    n2 = seq // n1
    row_groups = n2 // (KRON_DEPTH // n1)
    x4 = x.reshape(b, n1, n2, D_MODEL)
    kern = functools.partial(_fourier_ab_kernel, n1=n1, row_groups=row_groups, scale=1.0 / math.sqrt(seq * FOUR_GDIM))
    return pl.pallas_call(
        kern,
        grid=(b,),
        in_specs=[pl.BlockSpec((None, n1, n2, D_MODEL), lambda i: (i, 0, 0, 0)),
                  _const_spec((1, D_MODEL)),
                  pl.BlockSpec((D_MODEL // 2, FOUR_WIDTH), lambda i: (0, OFF_FV // FOUR_WIDTH),
                               pipeline_mode=pl.Buffered(1)),
                  _const_spec(chan.shape),
                  _const_spec((2, KRON_DEPTH, 2 * KRON_DEPTH)),
                  _const_spec(tw.shape),
                  _const_spec((n2, 2 * n2)),
                  pl.BlockSpec((1, N_MEM, D_MODEL), lambda i: (i, 0, 0)),
                  _const_spec((1, D_MODEL)),
                  _const_spec(w_kv_pk.shape)],
        out_specs=[pl.BlockSpec((None, n1, n2, FOUR_WIDTH), lambda i: (i, 0, 0, 0)),
                   pl.BlockSpec((1, N_MEM, 2 * ATTN_WIDTH), lambda i: (i, 0, 0))],
        out_shape=[jax.ShapeDtypeStruct((b, n1, n2, FOUR_WIDTH), BF16),
                   jax.ShapeDtypeStruct((b, N_MEM, 2 * ATTN_WIDTH), BF16)],
        scratch_shapes=[pltpu.VMEM((n1, n2, 2 * FOUR_WIDTH), BF16)],
        compiler_params=pltpu.CompilerParams(dimension_semantics=("arbitrary",), vmem_limit_bytes=VMEM_LIMIT),
        name="fourier_ab",
    )(x4, norm_in, w_in_pk, chan, stage_a, tw, stage_b, mem, norm_mem, w_kv_pk)


def _main_kernel(xm_ref, xp_ref, xq_ref, f_ref, kv_ref, gin_ref, win_ref, wpg_ref, psc_ref,
                 wpo_ref, wfo_ref, wao_ref, bg_ref, wo_ref, gf_ref, out_ref, uext_ref, *, seq, tile):
    i = pl.program_id(1)
    last = pl.num_programs(1) - 1
    gin = gin_ref[...]
    x = xm_ref[...]
    xn = _rmsnorm(x, gin).astype(BF16)

    def proj(off, width, lhs=xn):
        return _dot(lhs, _unpack_rows(win_ref[:, off:off + width]))

    def gate(branch):
        off = branch * D_MODEL
        return _sigmoid(proj(OFF_MG + off, D_MODEL) + bg_ref[:, off:off + D_MODEL])


    halo = jnp.concatenate([xp_ref[...], xq_ref[...]], axis=0)
    pv_ext = proj(OFF_PV, D_MODEL, jnp.concatenate([xn, _rmsnorm(halo, gin).astype(BF16)], axis=0))
    pv_prev = jnp.where(i > 0, pv_ext[tile:tile + POOL_HALO], 0.0)
    pv_next = jnp.where(i < last, pv_ext[tile + POOL_HALO:], 0.0)
    for s in range(D_MODEL // LANES):
        lanes = slice(s * LANES, (s + 1) * LANES)
        uext_ref[s, 0:POOL_HALO, :] = pv_prev[:, lanes]
        uext_ref[s, POOL_HALO:POOL_HALO + tile, :] = pv_ext[:tile, lanes]
        uext_ref[s, POOL_HALO + tile:, :] = pv_next[:, lanes]

    q = proj(OFF_Q, ATTN_WIDTH).astype(BF16)
    es, ls = [], []
    for h in range(ATTN_HEADS):
        kh = kv_ref[:, h * HEAD_DIM:(h + 1) * HEAD_DIM]
        s = lax.dot_general(q[:, h * HEAD_DIM:(h + 1) * HEAD_DIM], kh, (((1,), (1,)), ((), ())),
                            preferred_element_type=F32) * (1.0 / math.sqrt(HEAD_DIM))
        e = jnp.exp(s - jnp.max(s, axis=-1, keepdims=True))
        ls.append(jnp.sum(e, axis=-1, keepdims=True))
        es.append(e.astype(BF16))

    yb_in = (f_ref[...] * _silu(proj(OFF_FG, FOUR_WIDTH))).astype(BF16)

    silu_pg = _silu(proj(OFF_PG, D_MODEL))

    t = i * tile + lax.broadcasted_iota(jnp.int32, (tile, 1), 0)
    slabs_per_group = POOL_GDIM // LANES
    ys = []
    for g, w in enumerate(POOL_WINDOWS):
        half = w // 2
        inv_cnt = 1.0 / (jnp.minimum(t + half, seq) - jnp.maximum(t - half, 0)).astype(F32)
        ps = []
        for s in range(g * slabs_per_group, (g + 1) * slabs_per_group):
            win = uext_ref[s, pl.ds(POOL_HALO - half, tile, stride=1), :]
            for jj in range(1, w):
                win = win + uext_ref[s, pl.ds(POOL_HALO - half + jj, tile, stride=1), :]
            ps.append(win * inv_cnt - uext_ref[s, POOL_HALO:POOL_HALO + tile, :])
        p = jnp.concatenate(ps, axis=1).astype(BF16)
        ys.append(_dot(p, _unpack_rows(wpg_ref[g])))
    ya_in = (jnp.concatenate(ys, axis=1) * psc_ref[...] * silu_pg).astype(BF16)

    silu_ag = _silu(proj(OFF_AG, ATTN_WIDTH))
    o = jnp.concatenate(
        [_dot(es[h], kv_ref[:, ATTN_WIDTH + h * HEAD_DIM:ATTN_WIDTH + (h + 1) * HEAD_DIM]) / ls[h]
         for h in range(ATTN_HEADS)], axis=1)
    yc_in = (o * silu_ag).astype(BF16)

    merged = gate(0) * _dot(ya_in, _unpack_rows(wpo_ref[...]))
    merged = merged + gate(1) * _dot(yb_in, _unpack_rows(wfo_ref[...]))
    merged = merged + gate(2) * _dot(yc_in, _unpack_rows(wao_ref[...]))

    hres = x + _dot(merged.astype(BF16), _unpack_rows(wo_ref[...]))
    out_ref[...] = _rmsnorm(hres, gf_ref[...])


def _main_call(x, f, kv, norm_in, w_in_pk, w_pool_grp_pk, pool_scale,
               w_pool_out_pk, w_four_out_pk, w_attn_out_pk, b_gate, w_o_pk, norm_f):
    b, seq, _ = x.shape
    tile = TOKEN_TILE
    nt = seq // tile
    hb = tile // POOL_HALO
    nhb = seq // POOL_HALO
    kern = functools.partial(_main_kernel, seq=seq, tile=tile)
    return pl.pallas_call(
        kern,
        grid=(b, nt),
        in_specs=[pl.BlockSpec((None, tile, D_MODEL), lambda bi, i: (bi, i, 0)),
                  pl.BlockSpec((None, POOL_HALO, D_MODEL), lambda bi, i: (bi, jnp.maximum(i * hb - 1, 0), 0)),
                  pl.BlockSpec((None, POOL_HALO, D_MODEL),
                               lambda bi, i: (bi, jnp.minimum((i + 1) * hb, nhb - 1), 0)),
                  pl.BlockSpec((None, tile, FOUR_WIDTH), lambda bi, i: (bi, i, 0)),
                  pl.BlockSpec((None, N_MEM, 2 * ATTN_WIDTH), lambda bi, i: (bi, 0, 0)),
                  _const_spec((1, D_MODEL)),
                  _const_spec(w_in_pk.shape),
                  _const_spec(w_pool_grp_pk.shape),
                  _const_spec((1, D_MODEL)),
                  _const_spec(w_pool_out_pk.shape),
                  _const_spec(w_four_out_pk.shape),
                  _const_spec(w_attn_out_pk.shape),
                  _const_spec((1, 3 * D_MODEL)),
                  _const_spec(w_o_pk.shape),
                  _const_spec((1, D_MODEL))],
        out_specs=pl.BlockSpec((None, tile, D_MODEL), lambda bi, i: (bi, i, 0)),
        out_shape=jax.ShapeDtypeStruct((b, seq, D_MODEL), F32),
        scratch_shapes=[pltpu.VMEM((D_MODEL // LANES, tile + 2 * POOL_HALO, LANES), F32)],
        compiler_params=pltpu.CompilerParams(dimension_semantics=("arbitrary", "arbitrary"),
                                             vmem_limit_bytes=VMEM_LIMIT),
        name="encoder_main",
    )(x, x, x, f, kv, norm_in, w_in_pk, w_pool_grp_pk, pool_scale,
      w_pool_out_pk, w_four_out_pk, w_attn_out_pk, b_gate, w_o_pk, norm_f)


def _trunk(x, mem, norm_in, norm_mem, w_in_pk, w_pool_grp_pk, pool_scale, chan_pk, w_kv_pk,
           w_pool_out_pk, w_four_out_pk, w_attn_out_pk, b_gate, w_o_pk, norm_f):
    b, seq, _ = x.shape
    stage_a, tw, stage_b = _dft_constants(seq)
    stage_a, stage_b = (jnp.asarray(a).astype(BF16) for a in (stage_a, stage_b))
    if seq <= FOURIER_A_STEP_TOKENS:
        f, kv = _fourier_ab_call(x, norm_in, w_in_pk, chan_pk, stage_a, tw, stage_b, mem, norm_mem, w_kv_pk)
    else:
        kv = _kv_call(mem, norm_mem, w_kv_pk)
        t = _fourier_a_call(x, norm_in, w_in_pk, chan_pk, stage_a, tw)
        f = _fourier_b_call(t, stage_b, seq)
    f = jnp.swapaxes(f, 1, 2).reshape(b, seq, FOUR_WIDTH)
    return _main_call(x, f, kv, norm_in, w_in_pk, w_pool_grp_pk, pool_scale,
                      w_pool_out_pk, w_four_out_pk, w_attn_out_pk, b_gate, w_o_pk, norm_f)


def kernel(x_prompt, x_sample, mem_prompt, mem_sample, norm_in, norm_mem, w_in, w_pool_grp, pool_scale,
           w_four_grp, w_kv, w_pool_out, w_four_out, w_attn_out, b_gate, w_o, norm_f):
    assert norm_in.shape[0] == 1, "single-layer trunk"
    w_in_pk, w_pool_grp_pk, w_kv_pk, w_pool_out_pk, w_four_out_pk, w_attn_out_pk, w_o_pk = _pack_rows(
        w_in[0], w_pool_grp[0], w_kv[0], w_pool_out[0], w_four_out[0], w_attn_out[0], w_o[0])
    shared = (norm_in, norm_mem, w_in_pk, w_pool_grp_pk, pool_scale, _fold_chan_call(w_four_grp[0]), w_kv_pk,
              w_pool_out_pk, w_four_out_pk, w_attn_out_pk, b_gate, w_o_pk, norm_f.reshape(1, D_MODEL))
    return (_trunk(x_prompt, mem_prompt, *shared), _trunk(x_sample, mem_sample, *shared))
```

```python
import functools
import math

import jax
import jax.numpy as jnp
import numpy as np
from jax import lax
from jax.experimental import pallas as pl
from jax.experimental.pallas import tpu as pltpu

D_MODEL = 1024
EPS = 1e-6
POOL_WINDOWS = (2, 4, 8, 16)
POOL_GDIM = 256
POOL_HALO = 8
FOUR_GROUPS = 4
FOUR_GDIM = 128
FOUR_WIDTH = FOUR_GROUPS * FOUR_GDIM
ATTN_HEADS = 4
HEAD_DIM = 128
ATTN_WIDTH = ATTN_HEADS * HEAD_DIM
N_MEM = 256
OFF_PV, OFF_PG, OFF_FV, OFF_FG, OFF_Q, OFF_AG, OFF_MG = 0, 1024, 2048, 2560, 3072, 3584, 4096

SUBLANES = 8
LANES = 128
KRON_DEPTH = 256
FOURIER_A_STEP_TOKENS = 2048
FOURIER_B_STEP_TOKENS = 2048
TOKEN_TILE = 512
PACK_STEPS = 8
KV_BATCHES_PER_STEP = 4
VMEM_LIMIT = 56 * 1024 * 1024

BF16 = jnp.bfloat16
F32 = jnp.float32


def _dot(a, b):
    return jnp.dot(a, b, preferred_element_type=F32)


def _pack_kernel(*refs):
    n = len(refs) // 2
    for w_ref, o_ref in zip(refs[:n], refs[n:]):
        o_ref[...] = pltpu.bitcast(w_ref[...].astype(BF16), jnp.uint32)


def _pack_rows(*ws):
    flat = [w.reshape(-1, w.shape[-1]) for w in ws]
    in_specs, out_specs, out_shape = [], [], []
    for w in flat:
        k, n = w.shape
        in_specs.append(pl.BlockSpec((k // PACK_STEPS, n), lambda i: (i, 0)))
        out_specs.append(pl.BlockSpec((k // (2 * PACK_STEPS), n), lambda i: (i, 0)))
        out_shape.append(jax.ShapeDtypeStruct((k // 2, n), jnp.uint32))
    packed = pl.pallas_call(
        _pack_kernel,
        grid=(PACK_STEPS,),
        in_specs=in_specs,
        out_specs=out_specs,
        out_shape=out_shape,
        compiler_params=pltpu.CompilerParams(dimension_semantics=("arbitrary",)),
        name="pack_weights",
    )(*flat)
    return [p.reshape(*w.shape[:-2], w.shape[-2] // 2, w.shape[-1]) for p, w in zip(packed, ws)]


def _unpack_rows(w_u32):
    return pltpu.bitcast(w_u32, BF16)


def _rmsnorm(x, g):
    r = lax.rsqrt(jnp.mean(x * x, axis=-1, keepdims=True) + EPS)
    return x * r * g


def _sigmoid(x):
    return 0.5 * jnp.tanh(0.5 * x) + 0.5


def _silu(x):
    return x * _sigmoid(x)


def _const_spec(shape):
    nd = len(shape)
    return pl.BlockSpec(shape, lambda *_: (0,) * nd, pipeline_mode=pl.Buffered(1))


@functools.cache
def _channel_dft():
    c = np.arange(FOUR_GDIM)
    ang = 2.0 * np.pi * np.outer(c, c) / FOUR_GDIM
    return np.concatenate([np.cos(ang), -np.sin(ang)], axis=0).astype(np.float32)


def _dft_n1(seq):
    return min(KRON_DEPTH // SUBLANES, seq // (KRON_DEPTH // 2))


@functools.cache
def _dft_constants(seq):
    n1 = _dft_n1(seq)
    n2 = seq // n1
    group = KRON_DEPTH // n1
    k1 = np.arange(n1)
    psi = 2.0 * np.pi * np.outer(k1, k1) / n1
    eye = np.eye(group)
    mr = np.kron(np.cos(psi), eye)
    mi = np.kron(-np.sin(psi), eye)
    stage_a = np.stack([np.concatenate([mr, -mi], axis=1),
                        np.concatenate([mi, mr], axis=1)], axis=0)
    j = np.arange(n2 // group)
    r = np.arange(group)
    s2 = (group * j[:, None, None] + r[None, None, :])
    phi = 2.0 * np.pi * k1[None, :, None] * s2 / seq
    phi = phi.reshape(len(j), KRON_DEPTH, 1)
    tw = np.concatenate([np.broadcast_to(np.cos(phi), phi.shape[:2] + (LANES,)),
                         np.broadcast_to(-np.sin(phi), phi.shape[:2] + (LANES,))], axis=2)
    k2 = np.arange(n2)
    th = 2.0 * np.pi * np.outer(k2, k2) / n2
    stage_b = np.concatenate([np.cos(th), np.sin(th)], axis=1)
    return tuple(np.asarray(a, np.float32) for a in (stage_a, tw, stage_b))


def _fold_chan_kernel(c_ref, wg_ref, o_ref):
    for h in range(FOUR_GROUPS):
        r = jnp.dot(c_ref[...], wg_ref[h], precision=lax.Precision.HIGHEST, preferred_element_type=F32)
        folded = jnp.concatenate([r[:FOUR_GDIM], r[FOUR_GDIM:]], axis=1)
        o_ref[h] = pltpu.bitcast(folded.astype(BF16), jnp.uint32)


def _fold_chan_call(w_four_grp):
    return pl.pallas_call(
        _fold_chan_kernel,
        out_shape=jax.ShapeDtypeStruct((FOUR_GROUPS, FOUR_GDIM // 2, 2 * FOUR_GDIM), jnp.uint32),
        name="fold_chan",
    )(jnp.asarray(_channel_dft()), w_four_grp)


def _kv_kernel(mem_ref, g_ref, wkv_ref, kv_ref):
    nb = mem_ref.shape[0]
    mn = _rmsnorm(mem_ref[...].reshape(nb * N_MEM, D_MODEL), g_ref[...]).astype(BF16)
    kv_ref[...] = _dot(mn, _unpack_rows(wkv_ref[...])).astype(BF16).reshape(nb, N_MEM, 2 * ATTN_WIDTH)


def _kv_call(mem, norm_mem, w_kv_pk):
    b = mem.shape[0]
    per_step = math.gcd(b, KV_BATCHES_PER_STEP)
    return pl.pallas_call(
        _kv_kernel,
        grid=(b // per_step,),
        in_specs=[pl.BlockSpec((per_step, N_MEM, D_MODEL), lambda i: (i, 0, 0)),
                  _const_spec((1, D_MODEL)),
                  _const_spec(w_kv_pk.shape)],
        out_specs=pl.BlockSpec((per_step, N_MEM, 2 * ATTN_WIDTH), lambda i: (i, 0, 0)),
        out_shape=jax.ShapeDtypeStruct((b, N_MEM, 2 * ATTN_WIDTH), BF16),
        compiler_params=pltpu.CompilerParams(dimension_semantics=("arbitrary",)),
        name="kv_proj",
    )(mem, norm_mem, w_kv_pk)


def _stage_a(x_ref, g_ref, wfv_ref, chan_ref, sta_ref, tw_ref, n1, row_groups):
    group = KRON_DEPTH // n1
    step_rows = row_groups * group
    x = x_ref[...].reshape(n1 * step_rows, D_MODEL)
    xn = _rmsnorm(x, g_ref[...]).astype(BF16)
    fv = _dot(xn, _unpack_rows(wfv_ref[...])).astype(BF16)
    zr, zi = [], []
    for h in range(FOUR_GROUPS):
        z = _dot(fv[:, h * FOUR_GDIM:(h + 1) * FOUR_GDIM], _unpack_rows(chan_ref[h]))
        zr.append(z[:, :FOUR_GDIM])
        zi.append(z[:, FOUR_GDIM:])
    zr = jnp.concatenate(zr, axis=1).reshape(n1, step_rows, FOUR_WIDTH)
    zi = jnp.concatenate(zi, axis=1).reshape(n1, step_rows, FOUR_WIDTH)
    outs = []
    for g in range(row_groups):
        rs = slice(g * group, (g + 1) * group)
        zcat = jnp.concatenate([zr[:, rs, :].reshape(KRON_DEPTH, FOUR_WIDTH),
                                zi[:, rs, :].reshape(KRON_DEPTH, FOUR_WIDTH)], axis=0).astype(BF16)
        tr = _dot(sta_ref[0], zcat)
        ti = _dot(sta_ref[1], zcat)
        twr = tw_ref[g, :, :FOUR_GDIM]
        twi = tw_ref[g, :, FOUR_GDIM:]
        re, im = [], []
        for h in range(FOUR_GROUPS):
            a, b = tr[:, h * FOUR_GDIM:(h + 1) * FOUR_GDIM], ti[:, h * FOUR_GDIM:(h + 1) * FOUR_GDIM]
            re.append(a * twr - b * twi)
            im.append(a * twi + b * twr)
        outs.append(jnp.concatenate(re + im, axis=1).reshape(n1, group, 2 * FOUR_WIDTH))
    return jnp.concatenate(outs, axis=1).astype(BF16)


def _stage_b(t, stb, scale):
    tcat = jnp.concatenate([t[:, :FOUR_WIDTH], t[:, FOUR_WIDTH:]], axis=0)
    return (_dot(stb, tcat) * scale).astype(BF16)


def _fourier_a_kernel(x_ref, g_ref, wfv_ref, chan_ref, sta_ref, tw_ref, t_ref, *, n1, row_groups):
    t_ref[...] = _stage_a(x_ref, g_ref, wfv_ref, chan_ref, sta_ref, tw_ref, n1, row_groups)


def _fourier_a_call(x, norm_in, w_in_pk, chan, stage_a, tw):
    b, seq, _ = x.shape
    n1 = _dft_n1(seq)
    n2 = seq // n1
    step_rows = FOURIER_A_STEP_TOKENS // n1
    row_groups = step_rows // (KRON_DEPTH // n1)
    x4 = x.reshape(b, n1, n2, D_MODEL)
    return pl.pallas_call(
        functools.partial(_fourier_a_kernel, n1=n1, row_groups=row_groups),
        grid=(b, n2 // step_rows),
        in_specs=[pl.BlockSpec((None, n1, step_rows, D_MODEL), lambda i, j: (i, 0, j, 0)),
                  _const_spec((1, D_MODEL)),
                  pl.BlockSpec((D_MODEL // 2, FOUR_WIDTH), lambda i, j: (0, OFF_FV // FOUR_WIDTH),
                               pipeline_mode=pl.Buffered(1)),
                  _const_spec(chan.shape),
                  _const_spec((2, KRON_DEPTH, 2 * KRON_DEPTH)),
                  pl.BlockSpec((row_groups, KRON_DEPTH, 2 * FOUR_GDIM), lambda i, j: (j, 0, 0))],
        out_specs=pl.BlockSpec((None, n1, step_rows, 2 * FOUR_WIDTH), lambda i, j: (i, 0, j, 0)),
        out_shape=jax.ShapeDtypeStruct((b, n1, n2, 2 * FOUR_WIDTH), BF16),
        compiler_params=pltpu.CompilerParams(dimension_semantics=("arbitrary", "arbitrary")),
        name="fourier_a",
    )(x4, norm_in, w_in_pk, chan, stage_a, tw)


def _fourier_a_ring_call(x, norm_in, w_in_pk, chan, stage_a, tw):
    b, seq, _ = x.shape
    n1 = _dft_n1(seq)
    n2 = seq // n1
    step_rows = FOURIER_A_STEP_TOKENS // n1
    row_groups = step_rows // (KRON_DEPTH // n1)
    steps = n2 // step_rows
    x4 = x.reshape(b, n1, n2, D_MODEL)
    wfv = w_in_pk[:, OFF_FV:OFF_FV + FOUR_WIDTH]

    def outer(x_hbm, g_ref, wfv_ref, chan_ref, sta_ref, tw_hbm, t_hbm):
        def step(x_ref, tw_ref, t_ref):
            t_ref[0] = _stage_a(x_ref.at[0], g_ref, wfv_ref, chan_ref, sta_ref, tw_ref, n1, row_groups)

        pltpu.emit_pipeline(
            step,
            grid=(b, steps),
            in_specs=[pl.BlockSpec((1, n1, step_rows, D_MODEL), lambda i, j: (i, 0, j, 0),
                                   pipeline_mode=pl.Buffered(3)),
                      pl.BlockSpec((row_groups, KRON_DEPTH, 2 * FOUR_GDIM), lambda i, j: (j, 0, 0))],
            out_specs=[pl.BlockSpec((1, n1, step_rows, 2 * FOUR_WIDTH), lambda i, j: (i, 0, j, 0))],
        )(x_hbm, tw_hbm, t_hbm)

    vmem = pl.BlockSpec(memory_space=pltpu.VMEM)
    hbm = pl.BlockSpec(memory_space=pl.ANY)
    return pl.pallas_call(
        outer,
        in_specs=[hbm, vmem, vmem, vmem, vmem, hbm],
        out_specs=hbm,
        out_shape=jax.ShapeDtypeStruct((b, n1, n2, 2 * FOUR_WIDTH), BF16),
        compiler_params=pltpu.CompilerParams(vmem_limit_bytes=VMEM_LIMIT),
        name="fourier_a_ring",
    )(x4, norm_in, wfv, chan, stage_a, tw)


def _fourier_b_kernel(t_ref, stb_ref, f_ref, *, k1_per_step, scale):
    for k in range(k1_per_step):
        f_ref[k] = _stage_b(t_ref[k], stb_ref[...], scale)


def _fourier_b_call(t, stage_b, seq):
    b, n1, n2, _ = t.shape
    k1_per_step = max(1, FOURIER_B_STEP_TOKENS // n2)
    kern = functools.partial(_fourier_b_kernel, k1_per_step=k1_per_step,
                             scale=1.0 / math.sqrt(seq * FOUR_GDIM))
    return pl.pallas_call(
        kern,
        grid=(b, n1 // k1_per_step),
        in_specs=[pl.BlockSpec((None, k1_per_step, n2, 2 * FOUR_WIDTH), lambda i, j: (i, j, 0, 0)),
                  _const_spec((n2, 2 * n2))],
        out_specs=pl.BlockSpec((None, k1_per_step, n2, FOUR_WIDTH), lambda i, j: (i, j, 0, 0)),
        out_shape=jax.ShapeDtypeStruct((b, n1, n2, FOUR_WIDTH), BF16),
        compiler_params=pltpu.CompilerParams(dimension_semantics=("arbitrary", "arbitrary")),
        name="fourier_b",
    )(t, stage_b)


def _fourier_ab_kernel(x_ref, g_ref, wfv_ref, chan_ref, sta_ref, tw_ref, stb_ref, f_ref, t_ref, *, n1, row_groups, scale):
    t_ref[...] = _stage_a(x_ref, g_ref, wfv_ref, chan_ref, sta_ref, tw_ref, n1, row_groups)
    for k in range(n1):
        f_ref[k] = _stage_b(t_ref[k], stb_ref[...], scale)


def _fourier_ab_call(x, norm_in, w_in_pk, chan, stage_a, tw, stage_b):
    b, seq, _ = x.shape
    n1 = _dft_n1(seq)
    n2 = seq // n1
    row_groups = n2 // (KRON_DEPTH // n1)
    x4 = x.reshape(b, n1, n2, D_MODEL)
    kern = functools.partial(_fourier_ab_kernel, n1=n1, row_groups=row_groups, scale=1.0 / math.sqrt(seq * FOUR_GDIM))
    return pl.pallas_call(
        kern,
        grid=(b,),
        in_specs=[pl.BlockSpec((None, n1, n2, D_MODEL), lambda i: (i, 0, 0, 0)),
                  _const_spec((1, D_MODEL)),
                  pl.BlockSpec((D_MODEL // 2, FOUR_WIDTH), lambda i: (0, OFF_FV // FOUR_WIDTH),
                               pipeline_mode=pl.Buffered(1)),
                  _const_spec(chan.shape),
                  _const_spec((2, KRON_DEPTH, 2 * KRON_DEPTH)),
                  _const_spec(tw.shape),
                  _const_spec((n2, 2 * n2))],
        out_specs=pl.BlockSpec((None, n1, n2, FOUR_WIDTH), lambda i: (i, 0, 0, 0)),
        out_shape=jax.ShapeDtypeStruct((b, n1, n2, FOUR_WIDTH), BF16),
        scratch_shapes=[pltpu.VMEM((n1, n2, 2 * FOUR_WIDTH), BF16)],
        compiler_params=pltpu.CompilerParams(dimension_semantics=("arbitrary",), vmem_limit_bytes=VMEM_LIMIT),
        name="fourier_ab",
    )(x4, norm_in, w_in_pk, chan, stage_a, tw, stage_b)


def _main_kernel(xm_ref, xp_ref, xq_ref, f_ref, kv_ref, gin_ref, win_ref, wpg_ref, psc_ref,
                 wpo_ref, wfo_ref, wao_ref, bg_ref, wo_ref, gf_ref, out_ref, uext_ref, *, seq, tile):
    i = pl.program_id(1)
    last = pl.num_programs(1) - 1
    gin = gin_ref[...]
    x = xm_ref[...]
    xn = _rmsnorm(x, gin).astype(BF16)

    def proj(off, width, lhs=xn):
        return _dot(lhs, _unpack_rows(win_ref[:, off:off + width]))

    def gate(branch):
        off = branch * D_MODEL
        return _sigmoid(proj(OFF_MG + off, D_MODEL) + bg_ref[:, off:off + D_MODEL])


    halo = jnp.concatenate([xp_ref[...], xq_ref[...]], axis=0)
    pv_ext = proj(OFF_PV, D_MODEL, jnp.concatenate([xn, _rmsnorm(halo, gin).astype(BF16)], axis=0))
    pv_prev = jnp.where(i > 0, pv_ext[tile:tile + POOL_HALO], 0.0)
    pv_next = jnp.where(i < last, pv_ext[tile + POOL_HALO:], 0.0)
    for s in range(D_MODEL // LANES):
        lanes = slice(s * LANES, (s + 1) * LANES)
        uext_ref[s, 0:POOL_HALO, :] = pv_prev[:, lanes]
        uext_ref[s, POOL_HALO:POOL_HALO + tile, :] = pv_ext[:tile, lanes]
        uext_ref[s, POOL_HALO + tile:, :] = pv_next[:, lanes]

    q = proj(OFF_Q, ATTN_WIDTH).astype(BF16)
    es, ls = [], []
    for h in range(ATTN_HEADS):
        kh = kv_ref[:, h * HEAD_DIM:(h + 1) * HEAD_DIM]
        s = lax.dot_general(q[:, h * HEAD_DIM:(h + 1) * HEAD_DIM], kh, (((1,), (1,)), ((), ())),
                            preferred_element_type=F32) * (1.0 / math.sqrt(HEAD_DIM))
        e = jnp.exp(s - jnp.max(s, axis=-1, keepdims=True))
        ls.append(jnp.sum(e, axis=-1, keepdims=True))
        es.append(e.astype(BF16))

    yb_in = (f_ref[...] * _silu(proj(OFF_FG, FOUR_WIDTH))).astype(BF16)

    silu_pg = _silu(proj(OFF_PG, D_MODEL))

    t = i * tile + lax.broadcasted_iota(jnp.int32, (tile, 1), 0)
    slabs_per_group = POOL_GDIM // LANES
    ys = []
    for g, w in enumerate(POOL_WINDOWS):
        half = w // 2
        inv_cnt = 1.0 / (jnp.minimum(t + half, seq) - jnp.maximum(t - half, 0)).astype(F32)
        ps = []
        for s in range(g * slabs_per_group, (g + 1) * slabs_per_group):
            win = uext_ref[s, pl.ds(POOL_HALO - half, tile, stride=1), :]
            for jj in range(1, w):
                win = win + uext_ref[s, pl.ds(POOL_HALO - half + jj, tile, stride=1), :]
            ps.append(win * inv_cnt - uext_ref[s, POOL_HALO:POOL_HALO + tile, :])
        p = jnp.concatenate(ps, axis=1).astype(BF16)
        ys.append(_dot(p, _unpack_rows(wpg_ref[g])))
    ya_in = (jnp.concatenate(ys, axis=1) * psc_ref[...] * silu_pg).astype(BF16)

    silu_ag = _silu(proj(OFF_AG, ATTN_WIDTH))
    o = jnp.concatenate(
        [_dot(es[h], kv_ref[:, ATTN_WIDTH + h * HEAD_DIM:ATTN_WIDTH + (h + 1) * HEAD_DIM]) / ls[h]
         for h in range(ATTN_HEADS)], axis=1)
    yc_in = (o * silu_ag).astype(BF16)

    merged = gate(0) * _dot(ya_in, _unpack_rows(wpo_ref[...]))
    merged = merged + gate(1) * _dot(yb_in, _unpack_rows(wfo_ref[...]))
    merged = merged + gate(2) * _dot(yc_in, _unpack_rows(wao_ref[...]))

    hres = x + _dot(merged.astype(BF16), _unpack_rows(wo_ref[...]))
    out_ref[...] = _rmsnorm(hres, gf_ref[...])


def _main_call(x, f, kv, norm_in, w_in_pk, w_pool_grp_pk, pool_scale,
               w_pool_out_pk, w_four_out_pk, w_attn_out_pk, b_gate, w_o_pk, norm_f):
    b, seq, _ = x.shape
    tile = TOKEN_TILE
    nt = seq // tile
    hb = tile // POOL_HALO
    nhb = seq // POOL_HALO
    kern = functools.partial(_main_kernel, seq=seq, tile=tile)
    return pl.pallas_call(
        kern,
        grid=(b, nt),
        in_specs=[pl.BlockSpec((None, tile, D_MODEL), lambda bi, i: (bi, i, 0)),
                  pl.BlockSpec((None, POOL_HALO, D_MODEL), lambda bi, i: (bi, jnp.maximum(i * hb - 1, 0), 0)),
                  pl.BlockSpec((None, POOL_HALO, D_MODEL),
                               lambda bi, i: (bi, jnp.minimum((i + 1) * hb, nhb - 1), 0)),
                  pl.BlockSpec((None, tile, FOUR_WIDTH), lambda bi, i: (bi, i, 0)),
                  pl.BlockSpec((None, N_MEM, 2 * ATTN_WIDTH), lambda bi, i: (bi, 0, 0)),
                  _const_spec((1, D_MODEL)),
                  _const_spec(w_in_pk.shape),
                  _const_spec(w_pool_grp_pk.shape),
                  _const_spec((1, D_MODEL)),
                  _const_spec(w_pool_out_pk.shape),
                  _const_spec(w_four_out_pk.shape),
                  _const_spec(w_attn_out_pk.shape),
                  _const_spec((1, 3 * D_MODEL)),
                  _const_spec(w_o_pk.shape),
                  _const_spec((1, D_MODEL))],
        out_specs=pl.BlockSpec((None, tile, D_MODEL), lambda bi, i: (bi, i, 0)),
        out_shape=jax.ShapeDtypeStruct((b, seq, D_MODEL), F32),
        scratch_shapes=[pltpu.VMEM((D_MODEL // LANES, tile + 2 * POOL_HALO, LANES), F32)],
        compiler_params=pltpu.CompilerParams(dimension_semantics=("arbitrary", "arbitrary"),
                                             vmem_limit_bytes=VMEM_LIMIT),
        name="encoder_main",
    )(x, x, x, f, kv, norm_in, w_in_pk, w_pool_grp_pk, pool_scale,
      w_pool_out_pk, w_four_out_pk, w_attn_out_pk, b_gate, w_o_pk, norm_f)


def _trunk(x, mem, norm_in, norm_mem, w_in_pk, w_pool_grp_pk, pool_scale, chan_pk, w_kv_pk,
           w_pool_out_pk, w_four_out_pk, w_attn_out_pk, b_gate, w_o_pk, norm_f):
    b, seq, _ = x.shape
    stage_a, tw, stage_b = _dft_constants(seq)
    stage_a, stage_b = (jnp.asarray(a).astype(BF16) for a in (stage_a, stage_b))
    kv = _kv_call(mem, norm_mem, w_kv_pk)
    if seq <= FOURIER_A_STEP_TOKENS:
        f = _fourier_ab_call(x, norm_in, w_in_pk, chan_pk, stage_a, tw, stage_b)
    else:
        t = _fourier_a_ring_call(x, norm_in, w_in_pk, chan_pk, stage_a, tw)
        f = _fourier_b_call(t, stage_b, seq)
    f = jnp.swapaxes(f, 1, 2).reshape(b, seq, FOUR_WIDTH)
    return _main_call(x, f, kv, norm_in, w_in_pk, w_pool_grp_pk, pool_scale,
                      w_pool_out_pk, w_four_out_pk, w_attn_out_pk, b_gate, w_o_pk, norm_f)


def kernel(x_prompt, x_sample, mem_prompt, mem_sample, norm_in, norm_mem, w_in, w_pool_grp, pool_scale,
           w_four_grp, w_kv, w_pool_out, w_four_out, w_attn_out, b_gate, w_o, norm_f):
    assert norm_in.shape[0] == 1, "single-layer trunk"
    w_in_pk, w_pool_grp_pk, w_kv_pk, w_pool_out_pk, w_four_out_pk, w_attn_out_pk, w_o_pk = _pack_rows(
        w_in[0], w_pool_grp[0], w_kv[0], w_pool_out[0], w_four_out[0], w_attn_out[0], w_o[0])
    shared = (norm_in, norm_mem, w_in_pk, w_pool_grp_pk, pool_scale, _fold_chan_call(w_four_grp[0]), w_kv_pk,
              w_pool_out_pk, w_four_out_pk, w_attn_out_pk, b_gate, w_o_pk, norm_f.reshape(1, D_MODEL))
    return (_trunk(x_prompt, mem_prompt, *shared), _trunk(x_sample, mem_sample, *shared))
```
